```python
import math
import jax, jax.numpy as jnp
from jax import lax
import numpy as np

D_MODEL = 1024
BATCH = 8
SEQ = 2048
DEPTH = 1

D_MIX = D_MODEL
SSM_WIDTH = D_MIX // 2
SSM_GROUP = 16
SSM_GROUPS = SSM_WIDTH // SSM_GROUP
SSM_STATE = 64
MLA_HEADS = 4
QK_NOPE = 128
QK_ROPE = 64
QK_DIM = QK_NOPE + QK_ROPE
V_HEAD = 128
MLA_WIDTH = MLA_HEADS * V_HEAD
Q_LORA = 256
KV_LORA = 128
PLE_DIM = 256
ROPE_THETA = 10000.0
Q_BLOCK = 128
LN_EPS = 1e-5
RMS_EPS = 1e-6
DT_MIN = 1e-3
DT_MAX = 1e-1
MAX_POS_OFFSET = 4096
DEEPNORM_ALPHA = (2 * DEPTH) ** 0.25
DEEPNORM_BETA = (8 * DEPTH) ** -0.25

IN_SPLITS = (SSM_WIDTH, SSM_WIDTH, Q_LORA, KV_LORA, QK_ROPE, MLA_WIDTH)
D_IN = sum(IN_SPLITS)
IN_SPLIT_POINTS = tuple(int(v) for v in np.cumsum(IN_SPLITS)[:-1])

kernel_name = "hymba_s5_mla_deepnorm_ple"


def layer_norm(x, g, b):
    xf = x.astype(jnp.float32)
    mu = jnp.mean(xf, axis=-1, keepdims=True)
    var = jnp.mean(jnp.square(xf - mu), axis=-1, keepdims=True)
    return ((xf - mu) * lax.rsqrt(var + LN_EPS)).astype(x.dtype) * g + b


def rms_norm(x, g):
    xf = x.astype(jnp.float32)
    return (xf * lax.rsqrt(jnp.mean(xf * xf, axis=-1, keepdims=True) + RMS_EPS)).astype(x.dtype) * g


def rope_tables(positions):
    inv_freq = 1.0 / (ROPE_THETA ** (jnp.arange(0, QK_ROPE, 2, dtype=jnp.float32) / QK_ROPE))
    ang = positions.astype(jnp.float32)[..., None] * inv_freq
    return jnp.cos(ang), jnp.sin(ang)


def rotary(x, cos, sin):
    xf = x.astype(jnp.float32)
    x1, x2 = jnp.split(xf, 2, axis=-1)
    return jnp.concatenate([x1 * cos - x2 * sin, x2 * cos + x1 * sin], axis=-1).astype(x.dtype)


def s5_branch(u, a_re, a_im, log_dt, b_re, b_im, c_re, c_im, d_skip, w_glu, b_glu):
    bsz, seqlen, _ = u.shape
    f32 = jnp.float32
    uf = u.astype(f32)
    ug = uf.reshape(bsz, seqlen, SSM_GROUPS, SSM_GROUP)
    a_re = a_re.astype(f32)
    a_im = a_im.astype(f32)
    dt = jnp.exp(log_dt.astype(f32))[:, None]
    mag = jnp.exp(dt * a_re)
    ab_re = mag * jnp.cos(dt * a_im)
    ab_im = mag * jnp.sin(dt * a_im)
    den = a_re * a_re + a_im * a_im
    nr = ab_re - 1.0
    k_re = (nr * a_re + ab_im * a_im) / den
    k_im = (ab_im * a_re - nr * a_im) / den
    bu_re = jnp.einsum('gpc,blgc->lbgp', b_re.astype(f32), ug)
    bu_im = jnp.einsum('gpc,blgc->lbgp', b_im.astype(f32), ug)
    x_re = k_re * bu_re - k_im * bu_im
    x_im = k_re * bu_im + k_im * bu_re
    a_seq_re = jnp.broadcast_to(ab_re, (seqlen, 1, SSM_GROUPS, SSM_STATE))
    a_seq_im = jnp.broadcast_to(ab_im, (seqlen, 1, SSM_GROUPS, SSM_STATE))

    def combine(left, right):
        al_re, al_im, bl_re, bl_im = left
        ar_re, ar_im, br_re, br_im = right
        return (al_re * ar_re - al_im * ar_im,
                al_re * ar_im + al_im * ar_re,
                ar_re * bl_re - ar_im * bl_im + br_re,
                ar_re * bl_im + ar_im * bl_re + br_im)

    _, _, h_re, h_im = lax.associative_scan(combine, (a_seq_re, a_seq_im, x_re, x_im), axis=0)
    y = (jnp.einsum('gcp,lbgp->blgc', c_re.astype(f32), h_re)
         - jnp.einsum('gcp,lbgp->blgc', c_im.astype(f32), h_im))
    y = y.reshape(bsz, seqlen, SSM_WIDTH) + d_skip.astype(f32) * uf
    y = jax.nn.gelu(y)
    y = y * jax.nn.sigmoid(y @ w_glu.astype(f32) + b_glu.astype(f32))
    return y.astype(u.dtype)


def mla_branch(cq, ckv, kr, cos, sin, q_norm_g, w_uq, kv_norm_g, w_ukv):
    bsz, seqlen, _ = cq.shape
    q = (rms_norm(cq, q_norm_g) @ w_uq).reshape(bsz, seqlen, MLA_HEADS, QK_DIM)
    q = jnp.concatenate([q[..., :QK_NOPE],
                         rotary(q[..., QK_NOPE:], cos[:, :, None, :], sin[:, :, None, :])], axis=-1)
    kv = (rms_norm(ckv, kv_norm_g) @ w_ukv).reshape(bsz, seqlen, MLA_HEADS, QK_NOPE + V_HEAD)
    k_nope, v = kv[..., :QK_NOPE], kv[..., QK_NOPE:]
    k_rope = rotary(kr, cos, sin)
    k = jnp.concatenate([k_nope, jnp.broadcast_to(k_rope[:, :, None, :],
                                                  (bsz, seqlen, MLA_HEADS, QK_ROPE))], axis=-1)
    scale = QK_DIM ** -0.5
    n_blocks = seqlen // Q_BLOCK
    q_blocks = q.reshape(bsz, n_blocks, Q_BLOCK, MLA_HEADS, QK_DIM).transpose(1, 0, 2, 3, 4)
    key_pos = jnp.arange(seqlen)

    def attend(args):
        qb, start = args
        s = jnp.einsum('bqhd,bkhd->bhqk', qb, k, preferred_element_type=jnp.float32) * scale
        q_pos = start + jnp.arange(Q_BLOCK)
        causal = key_pos[None, :] <= q_pos[:, None]
        s = jnp.where(causal, s, -jnp.inf)
        pr = jax.nn.softmax(s, axis=-1).astype(v.dtype)
        return jnp.einsum('bhqk,bkhd->bqhd', pr, v)

    out = lax.map(attend, (q_blocks, jnp.arange(n_blocks) * Q_BLOCK))
    return out.transpose(1, 0, 2, 3, 4).reshape(bsz, seqlen, MLA_WIDTH)


def setup_inputs(seed: int = 0) -> dict:
    key = jax.random.key(seed)
    ks = jax.random.split(key, 26)
    f32 = jnp.float32
    nrm = lambda k, shape, s: jax.random.normal(k, shape, f32) * s
    x = jax.random.normal(ks[0], (BATCH, SEQ, D_MODEL), f32)
    p = jax.random.normal(ks[1], (DEPTH, BATCH, SEQ, PLE_DIM), f32)
    offsets = jax.random.randint(ks[2], (BATCH, 1), 0, MAX_POS_OFFSET, dtype=jnp.int32)
    positions = (jnp.arange(SEQ, dtype=jnp.int32)[None, :] + offsets).astype(jnp.int32)
    ln_emb_g = 1.0 + nrm(ks[3], (D_MODEL,), 0.05)
    ln_emb_b = nrm(ks[4], (D_MODEL,), 0.02)
    w_in = nrm(ks[5], (DEPTH, D_MODEL, D_IN), D_MODEL ** -0.5)
    a_re = -0.5 * jnp.exp(nrm(ks[6], (DEPTH, SSM_GROUPS, SSM_STATE), 0.05))
    a_im = math.pi * jnp.arange(SSM_STATE, dtype=f32) + nrm(ks[7], (DEPTH, SSM_GROUPS, SSM_STATE), 0.05)
    log_dt = jax.random.uniform(ks[8], (DEPTH, SSM_GROUPS), f32, math.log(DT_MIN), math.log(DT_MAX))
    b_re = nrm(ks[9], (DEPTH, SSM_GROUPS, SSM_STATE, SSM_GROUP), (2 * SSM_GROUP) ** -0.5)
    b_im = nrm(ks[10], (DEPTH, SSM_GROUPS, SSM_STATE, SSM_GROUP), (2 * SSM_GROUP) ** -0.5)
    c_re = nrm(ks[11], (DEPTH, SSM_GROUPS, SSM_GROUP, SSM_STATE), SSM_STATE ** -0.5)
    c_im = nrm(ks[12], (DEPTH, SSM_GROUPS, SSM_GROUP, SSM_STATE), SSM_STATE ** -0.5)
    d_skip = nrm(ks[13], (DEPTH, SSM_WIDTH), 1.0)
    w_glu = nrm(ks[14], (DEPTH, SSM_WIDTH, SSM_WIDTH), SSM_WIDTH ** -0.5)
    b_glu = nrm(ks[15], (DEPTH, SSM_WIDTH), 0.02)
    q_norm_g = 1.0 + nrm(ks[16], (DEPTH, Q_LORA), 0.05)
    w_uq = nrm(ks[17], (DEPTH, Q_LORA, MLA_HEADS * QK_DIM), Q_LORA ** -0.5)
    kv_norm_g = 1.0 + nrm(ks[18], (DEPTH, KV_LORA), 0.05)
    w_ukv = nrm(ks[19], (DEPTH, KV_LORA, MLA_HEADS * (QK_NOPE + V_HEAD)), KV_LORA ** -0.5)
    w_out = nrm(ks[20], (DEPTH, D_MIX, D_MODEL), D_MIX ** -0.5 * DEEPNORM_BETA)
    w_pg = nrm(ks[21], (DEPTH, D_MODEL, D_MODEL), D_MODEL ** -0.5)
    w_pp = nrm(ks[22], (DEPTH, PLE_DIM, D_MODEL), PLE_DIM ** -0.5 * DEEPNORM_BETA)
    ln_g = 1.0 + nrm(ks[23], (DEPTH, D_MODEL), 0.05)
    ln_b = nrm(ks[24], (DEPTH, D_MODEL), 0.02)
    return {"x": x, "p": p, "positions": positions, "ln_emb_g": ln_emb_g, "ln_emb_b": ln_emb_b,
            "w_in": w_in, "a_re": a_re, "a_im": a_im, "log_dt": log_dt,
            "b_re": b_re, "b_im": b_im, "c_re": c_re, "c_im": c_im, "d_skip": d_skip,
            "w_glu": w_glu, "b_glu": b_glu, "q_norm_g": q_norm_g, "w_uq": w_uq,
            "kv_norm_g": kv_norm_g, "w_ukv": w_ukv, "w_out": w_out, "w_pg": w_pg,
            "w_pp": w_pp, "ln_g": ln_g, "ln_b": ln_b}


def reference(x, p, positions, ln_emb_g, ln_emb_b, w_in, a_re, a_im, log_dt,
              b_re, b_im, c_re, c_im, d_skip, w_glu, b_glu, q_norm_g, w_uq,
              kv_norm_g, w_ukv, w_out, w_pg, w_pp, ln_g, ln_b):
    x = layer_norm(x, ln_emb_g, ln_emb_b)
    cos, sin = rope_tables(positions)
    for i in range(DEPTH):
        z = x @ w_in[i]
        xs, gs, cq, ckv, kr, gm = jnp.split(z, IN_SPLIT_POINTS, axis=-1)
        ys = s5_branch(xs, a_re[i], a_im[i], log_dt[i], b_re[i], b_im[i], c_re[i], c_im[i],
                       d_skip[i], w_glu[i], b_glu[i]) * jax.nn.silu(gs)
        ym = mla_branch(cq, ckv, kr, cos, sin, q_norm_g[i], w_uq[i], kv_norm_g[i],
                        w_ukv[i]) * jax.nn.silu(gm)
        mix = jnp.concatenate([ys, ym], axis=-1) @ w_out[i]
        u = DEEPNORM_ALPHA * x + mix
        ple = jax.nn.sigmoid(u @ w_pg[i]) * (p[i] @ w_pp[i])
        x = layer_norm(u + ple, ln_g[i], ln_b[i])
    return x
```

```python
import functools
import math

import numpy as np
import jax
import jax.numpy as jnp
from jax import lax
from jax.experimental import pallas as pl
from jax.experimental.pallas import tpu as pltpu

F32 = jnp.float32
BF16 = jnp.bfloat16

SSM_GROUP = 16
SSM_STATE = 64
MLA_HEADS = 4
QK_NOPE = 128
QK_ROPE = 64
V_HEAD = 128
Q_LORA = 256
KV_LORA = 128
ROPE_THETA = 10000.0
LN_EPS = 1e-5
RMS_EPS = 1e-6
DEPTH = 1
DEEPNORM_ALPHA = (2 * DEPTH) ** 0.25

LANES = 128
HEAD_PAD = 2 * LANES
S5_CHUNK = 16
VMEM_LIMIT = 56 * 1024 * 1024


def _layer_norm(x, g, b):
    mu = jnp.mean(x, axis=-1, keepdims=True)
    xc = x - mu
    var = jnp.mean(xc * xc, axis=-1, keepdims=True)
    return xc * lax.rsqrt(var + LN_EPS) * g + b


def _rms_norm(x, g):
    return x * lax.rsqrt(jnp.mean(x * x, axis=-1, keepdims=True) + RMS_EPS) * g


def _rotate_pair_tile(t, c2, lane):
    p = t * c2
    return jnp.where(lane < QK_ROPE, p + pltpu.roll(p, QK_ROPE, axis=1), 0.0)


def _in_proj_kernel(x_ref, pos_ref, invf_ref, lng_ref, lnb_ref, win_ref, qg_ref, wq_ref,
                    kvg_ref, wkv_ref, xs_ref, gs_ref, gm_ref, q_ref, k_ref, v_ref, *, q_scale):
    sw = xs_ref.shape[1]
    xn = _layer_norm(x_ref[...], lng_ref[...], lnb_ref[...]).astype(BF16)
    xs_ref[...] = jnp.dot(xn, win_ref[:, 0:sw], preferred_element_type=F32).astype(BF16)
    gs_ref[...] = jnp.dot(xn, win_ref[:, sw:2 * sw], preferred_element_type=F32).astype(BF16)
    gm_ref[...] = jnp.dot(xn, win_ref[:, 2 * sw:3 * sw], preferred_element_type=F32).astype(BF16)
    lat = jnp.dot(xn, win_ref[:, 3 * sw:], preferred_element_type=F32)
    cq = lat[:, 0:Q_LORA]
    ckv = lat[:, Q_LORA:Q_LORA + KV_LORA]
    krt = lat[:, Q_LORA + KV_LORA:]

    ang = pos_ref[...].astype(F32) * invf_ref[...]
    lane = lax.broadcasted_iota(jnp.int32, ang.shape, 1)
    sn = jnp.sin(ang)
    c2 = jnp.where(lane < QK_ROPE, jnp.cos(ang), jnp.where(lane < QK_ROPE + QK_ROPE // 2, -sn, sn))

    cqn = _rms_norm(cq, qg_ref[...] * q_scale).astype(BF16)
    qf = jnp.dot(cqn, wq_ref[...], preferred_element_type=F32)
    for h in range(MLA_HEADS):
        o = h * HEAD_PAD
        q_ref[:, o:o + QK_NOPE] = qf[:, o:o + QK_NOPE].astype(BF16)
        q_ref[:, o + QK_NOPE:o + HEAD_PAD] = _rotate_pair_tile(
            qf[:, o + QK_NOPE:o + HEAD_PAD], c2, lane).astype(BF16)

    ckvn = _rms_norm(ckv, kvg_ref[...]).astype(BF16)
    kvf = jnp.dot(ckvn, wkv_ref[...], preferred_element_type=F32)
    k_rope = _rotate_pair_tile(krt, c2, lane).astype(BF16)
    nv = MLA_HEADS * QK_NOPE
    v_ref[...] = kvf[:, nv:].astype(BF16)
    for h in range(MLA_HEADS):
        o = h * HEAD_PAD
        k_ref[:, o:o + QK_NOPE] = kvf[:, h * QK_NOPE:(h + 1) * QK_NOPE].astype(BF16)
        k_ref[:, o + QK_NOPE:o + HEAD_PAD] = k_rope


def _in_proj_call(x2, pos_b, invf, lng, lnb, win, qg, wq, kvg, wkv, *, tm):
    n, d = x2.shape
    sw = (win.shape[1] - (Q_LORA + KV_LORA + 2 * QK_ROPE)) // 3
    row = lambda w: pl.BlockSpec((tm, w), lambda i: (i, 0))
    full = lambda a: pl.BlockSpec(a.shape, lambda i: (0,) * a.ndim)
    qk_w = MLA_HEADS * HEAD_PAD
    out_shape = (
        jax.ShapeDtypeStruct((n, sw), BF16), jax.ShapeDtypeStruct((n, sw), BF16),
        jax.ShapeDtypeStruct((n, sw), BF16), jax.ShapeDtypeStruct((n, qk_w), BF16),
        jax.ShapeDtypeStruct((n, qk_w), BF16), jax.ShapeDtypeStruct((n, MLA_HEADS * V_HEAD), BF16))
    return pl.pallas_call(
        functools.partial(_in_proj_kernel, q_scale=(QK_NOPE + QK_ROPE) ** -0.5),
        grid=(n // tm,),
        in_specs=[row(d), row(LANES), full(invf), full(lng), full(lnb), full(win), full(qg),
                  full(wq), full(kvg), full(wkv)],
        out_specs=(row(sw), row(sw), row(sw), row(qk_w), row(qk_w), row(MLA_HEADS * V_HEAD)),
        out_shape=out_shape,
        compiler_params=pltpu.CompilerParams(dimension_semantics=("arbitrary",),
                                             vmem_limit_bytes=VMEM_LIMIT),
        name="in_proj",
    )(x2, pos_b, invf, lng, lnb, win, qg, wq, kvg, wkv)


def _s5_kernel(u_ref, t_ref, wst_ref, wo_ref, ar_ref, ai_ref, ais_ref, y_ref,
               s_ref, ssw_ref, hp_ref, *, gb, n_chunks, bsz):
    for g in range(gb):
        s = jnp.dot(u_ref[g], wst_ref[g], preferred_element_type=F32)
        s_ref[g] = s
        ssw_ref[g] = pltpu.roll(s, SSM_STATE, axis=1)

    ar = [ar_ref[g] for g in range(gb)]
    ai = [ai_ref[g] for g in range(gb)]
    ais = [ais_ref[g] for g in range(gb)]

    def step(c, carry):
        r = pl.multiple_of(c * bsz, bsz)
        new = []
        for g in range(gb):
            h, hsw = carry[2 * g], carry[2 * g + 1]
            hp_ref[g, pl.ds(r, bsz), :] = h
            new.append(ar[g] * h + ai[g] * hsw + s_ref[g, pl.ds(r, bsz), :])
            new.append(ar[g] * hsw + ais[g] * h + ssw_ref[g, pl.ds(r, bsz), :])
        return tuple(new)

    zero = jnp.zeros((bsz, 2 * SSM_STATE), F32)
    lax.fori_loop(0, n_chunks, step, (zero,) * (2 * gb))

    for g in range(gb):
        y = jnp.dot(u_ref[g], t_ref[g], preferred_element_type=F32)
        y = y + jnp.dot(hp_ref[g].astype(BF16), wo_ref[g], preferred_element_type=F32)
        y_ref[g] = y.astype(BF16)


def _s5_call(u, tmat, wst, wo, ar2, ai2, ai2s, *, bsz, gb):
    groups, rows, kw = u.shape
    n_chunks = rows // bsz
    blk = lambda a: pl.BlockSpec((gb,) + a.shape[1:], lambda i: (i,) + (0,) * (a.ndim - 1))
    return pl.pallas_call(
        functools.partial(_s5_kernel, gb=gb, n_chunks=n_chunks, bsz=bsz),
        grid=(groups // gb,),
        in_specs=[blk(u), blk(tmat), blk(wst), blk(wo), blk(ar2), blk(ai2), blk(ai2s)],
        out_specs=blk(u),
        out_shape=jax.ShapeDtypeStruct(u.shape, BF16),
        scratch_shapes=[pltpu.VMEM((gb, rows, 2 * SSM_STATE), F32),
                        pltpu.VMEM((gb, rows, 2 * SSM_STATE), F32),
                        pltpu.VMEM((gb, rows, 2 * SSM_STATE), F32)],
        compiler_params=pltpu.CompilerParams(dimension_semantics=("arbitrary",),
                                             vmem_limit_bytes=VMEM_LIMIT),
        name="s5",
    )(u, tmat, wst, wo, ar2, ai2, ai2s)


def _s5_matrices(a_re, a_im, log_dt, b_re, b_im, c_re, c_im):
    hi = lax.Precision.HIGHEST
    tc = S5_CHUNK
    dt = jnp.exp(log_dt)[:, None]
    mag = jnp.exp(dt * a_re)
    ab_re = mag * jnp.cos(dt * a_im)
    ab_im = mag * jnp.sin(dt * a_im)
    den = a_re * a_re + a_im * a_im
    nr = ab_re - 1.0
    k_re = (nr * a_re + ab_im * a_im) / den
    k_im = (ab_im * a_re - nr * a_im) / den
    taus = jnp.arange(tc + 1, dtype=F32)[:, None, None]
    pmag = jnp.exp(taus * (dt * a_re))
    pw_re = pmag * jnp.cos(taus * (dt * a_im))
    pw_im = pmag * jnp.sin(taus * (dt * a_im))
    bk_re = k_re[..., None] * b_re - k_im[..., None] * b_im
    bk_im = k_re[..., None] * b_im + k_im[..., None] * b_re
    w_re = pw_re[..., None] * bk_re - pw_im[..., None] * bk_im
    w_im = pw_re[..., None] * bk_im + pw_im[..., None] * bk_re
    m = (jnp.einsum('gop,tgpi->tgoi', c_re, w_re[:tc], precision=hi)
         - jnp.einsum('gop,tgpi->tgoi', c_im, w_im[:tc], precision=hi))
    s_idx = jnp.arange(tc)[:, None]
    t_idx = jnp.arange(tc)[None, :]
    lag = t_idx - s_idx
    mt = jnp.where((lag >= 0)[:, :, None, None, None], m[jnp.clip(lag, 0, tc - 1)], 0.0)
    g = a_re.shape[0]
    tmat = mt.transpose(2, 0, 4, 1, 3).reshape(g, tc * SSM_GROUP, tc * SSM_GROUP)
    wr = w_re[tc - 1 - jnp.arange(tc)]
    wi = w_im[tc - 1 - jnp.arange(tc)]
    wst = jnp.concatenate([wr.transpose(1, 0, 3, 2), wi.transpose(1, 0, 3, 2)], axis=-1)
    wst = wst.reshape(g, tc * SSM_GROUP, 2 * SSM_STATE)
    e_re = c_re[None] * pw_re[1:, :, None, :] - c_im[None] * pw_im[1:, :, None, :]
    e_im = c_re[None] * pw_im[1:, :, None, :] + c_im[None] * pw_re[1:, :, None, :]
    wo = jnp.concatenate([e_re, -e_im], axis=-1)
    wo = wo.transpose(1, 3, 0, 2).reshape(g, 2 * SSM_STATE, tc * SSM_GROUP)
    ar, ai = pw_re[tc], pw_im[tc]
    ar2 = jnp.concatenate([ar, ar], axis=-1)[:, None, :]
    ai2 = jnp.concatenate([-ai, ai], axis=-1)[:, None, :]
    return tmat.astype(BF16), wst.astype(BF16), wo.astype(BF16), ar2, ai2, -ai2


def _attn_kernel(q_ref, k_ref, v_ref, o_ref, *, tq):
    i = pl.program_id(2)
    q = q_ref[...]

    def block(j, carry, masked):
        m, l, acc = carry
        r = pl.multiple_of(j * tq, tq)
        k = k_ref[pl.ds(r, tq), :]
        v = v_ref[pl.ds(r, tq), :]
        s = lax.dot_general(q, k, (((1,), (1,)), ((), ())), preferred_element_type=F32)
        if masked:
            qi = lax.broadcasted_iota(jnp.int32, s.shape, 0)
            ki = lax.broadcasted_iota(jnp.int32, s.shape, 1)
            s = jnp.where(ki <= qi, s, -jnp.inf)
        m_new = jnp.maximum(m, jnp.max(s, axis=-1, keepdims=True))
        alpha = jnp.exp(m - m_new)
        p = jnp.exp(s - m_new)
        l = alpha * l + jnp.sum(p, axis=-1, keepdims=True)
        acc = alpha * acc + jnp.dot(p.astype(BF16), v, preferred_element_type=F32)
        return m_new, l, acc

    init = (jnp.full((tq, 1), -jnp.inf, F32), jnp.zeros((tq, 1), F32),
            jnp.zeros((tq, V_HEAD), F32))
    carry = lax.fori_loop(0, i, lambda j, c: block(j, c, False), init)
    m, l, acc = block(i, carry, True)
    o_ref[...] = (acc / l).astype(BF16)


def _attn_call(q, k, v, *, bsz, seq, tq):
    nq = seq // tq
    return pl.pallas_call(
        functools.partial(_attn_kernel, tq=tq),
        grid=(bsz, MLA_HEADS, nq),
        in_specs=[pl.BlockSpec((tq, HEAD_PAD), lambda b, h, i: (b * nq + i, h)),
                  pl.BlockSpec((seq, HEAD_PAD), lambda b, h, i: (b, h)),
                  pl.BlockSpec((seq, V_HEAD), lambda b, h, i: (b, h))],
        out_specs=pl.BlockSpec((tq, V_HEAD), lambda b, h, i: (b * nq + i, h)),
        out_shape=jax.ShapeDtypeStruct((bsz * seq, MLA_HEADS * V_HEAD), BF16),
        compiler_params=pltpu.CompilerParams(
            dimension_semantics=("arbitrary", "arbitrary", "arbitrary"),
            vmem_limit_bytes=VMEM_LIMIT),
        name="attn",
    )(q, k, v)


def _gelu_tanh(x):
    return 0.5 * x * (1.0 + jnp.tanh(math.sqrt(2.0 / math.pi) * (x + 0.044715 * (x * x * x))))


def _out_kernel(x_ref, p_ref, yssm_ref, xs_ref, gs_ref, o_ref, gm_ref, lng0_ref, lnb0_ref,
                dskip_ref, wglu_ref, bglu_ref, wout_ref, wpg_ref, wpp_ref, lng_ref, lnb_ref,
                out_ref):
    sw = xs_ref.shape[1]
    xn = _layer_norm(x_ref[...], lng0_ref[...], lnb0_ref[...])
    y = yssm_ref[...].astype(F32) + dskip_ref[...] * xs_ref[...].astype(F32)
    y = _gelu_tanh(y)
    glu = jnp.dot(y.astype(BF16), wglu_ref[...], preferred_element_type=F32) + bglu_ref[...]
    gs = gs_ref[...].astype(F32)
    ys = y * jax.nn.sigmoid(glu) * (gs * jax.nn.sigmoid(gs))
    gm = gm_ref[...].astype(F32)
    ym = o_ref[...].astype(F32) * (gm * jax.nn.sigmoid(gm))
    mix = (jnp.dot(ys.astype(BF16), wout_ref[0:sw, :], preferred_element_type=F32)
           + jnp.dot(ym.astype(BF16), wout_ref[sw:, :], preferred_element_type=F32))
    u = DEEPNORM_ALPHA * xn + mix
    gate = jax.nn.sigmoid(jnp.dot(u.astype(BF16), wpg_ref[...], preferred_element_type=F32))
    ple = gate * jnp.dot(p_ref[...].astype(BF16), wpp_ref[...], preferred_element_type=F32)
    out_ref[...] = _layer_norm(u + ple, lng_ref[...], lnb_ref[...])


def _out_call(x2, p2, yssm, xs, gs, o, gm, lng0, lnb0, dskip, wglu, bglu, wout, wpg, wpp,
              lng, lnb, *, tm):
    n, d = x2.shape
    row = lambda a: pl.BlockSpec((tm, a.shape[1]), lambda i: (i, 0))
    full = lambda a: pl.BlockSpec(a.shape, lambda i: (0,) * a.ndim)
    rows = (x2, p2, yssm, xs, gs, o, gm)
    params = (lng0, lnb0, dskip, wglu, bglu, wout, wpg, wpp, lng, lnb)
    return pl.pallas_call(
        _out_kernel,
        grid=(n // tm,),
        in_specs=[row(a) for a in rows] + [full(a) for a in params],
        out_specs=pl.BlockSpec((tm, d), lambda i: (i, 0)),
        out_shape=jax.ShapeDtypeStruct((n, d), F32),
        compiler_params=pltpu.CompilerParams(dimension_semantics=("arbitrary",),
                                             vmem_limit_bytes=VMEM_LIMIT),
        name="out_proj",
    )(*rows, *params)


def _swap_halves_cols(w):
    half = w.shape[-1] // 2
    return jnp.concatenate([w[..., half:], w[..., :half]], axis=-1)


def kernel(x, p, positions, ln_emb_g, ln_emb_b, w_in, a_re, a_im, log_dt, b_re, b_im, c_re,
           c_im, d_skip, w_glu, b_glu, q_norm_g, w_uq, kv_norm_g, w_ukv, w_out, w_pg, w_pp,
           ln_g, ln_b):
    bsz, seq, d = x.shape
    n = bsz * seq
    sw = d_skip.shape[-1]
    groups = sw // SSM_GROUP
    mw = MLA_HEADS * V_HEAD
    row2 = lambda a: a.reshape(1, -1)

    wi = w_in[0]
    o_cq = 2 * sw
    o_ckv = o_cq + Q_LORA
    o_kr = o_ckv + KV_LORA
    o_gm = o_kr + QK_ROPE
    kr_cols = wi[:, o_kr:o_gm]
    win = jnp.concatenate([wi[:, 0:sw], wi[:, sw:2 * sw], wi[:, o_gm:o_gm + mw],
                           wi[:, o_cq:o_ckv], wi[:, o_ckv:o_kr], kr_cols,
                           _swap_halves_cols(kr_cols)], axis=1).astype(BF16)
    wq3 = w_uq[0].reshape(Q_LORA, MLA_HEADS, QK_NOPE + QK_ROPE)
    wq = jnp.concatenate([wq3, _swap_halves_cols(wq3[..., QK_NOPE:])], axis=-1)
    wq = wq.reshape(Q_LORA, MLA_HEADS * HEAD_PAD).astype(BF16)
    wkv3 = w_ukv[0].reshape(KV_LORA, MLA_HEADS, QK_NOPE + V_HEAD)
    wkv = jnp.concatenate([wkv3[..., :QK_NOPE].reshape(KV_LORA, -1),
                           wkv3[..., QK_NOPE:].reshape(KV_LORA, -1)], axis=1).astype(BF16)

    inv_freq = 1.0 / (ROPE_THETA ** (np.arange(0, QK_ROPE, 2, dtype=np.float32) / QK_ROPE))
    invf = jnp.asarray(np.tile(inv_freq, LANES // inv_freq.size).reshape(1, LANES), F32)
    pos_b = jnp.broadcast_to(positions.reshape(n, 1), (n, LANES))

    x2 = x.reshape(n, d)
    xs, gs, gm, q, k, v = _in_proj_call(
        x2, pos_b, invf, row2(ln_emb_g), row2(ln_emb_b), win, row2(q_norm_g[0]), wq,
        row2(kv_norm_g[0]), wkv, tm=512)

    n_chunks = seq // S5_CHUNK
    u = xs.reshape(bsz, n_chunks, S5_CHUNK, groups, SSM_GROUP).transpose(3, 1, 0, 2, 4)
    u = u.reshape(groups, n_chunks * bsz, S5_CHUNK * SSM_GROUP)
    tmat, wst, wo, ar2, ai2, ai2s = _s5_matrices(a_re[0], a_im[0], log_dt[0], b_re[0], b_im[0],
                                                 c_re[0], c_im[0])
    yg = _s5_call(u, tmat, wst, wo, ar2, ai2, ai2s, bsz=bsz, gb=4)
    yssm = yg.reshape(groups, n_chunks, bsz, S5_CHUNK, SSM_GROUP).transpose(2, 1, 3, 0, 4)
    yssm = yssm.reshape(n, sw)

    o = _attn_call(q, k, v, bsz=bsz, seq=seq, tq=256)

    out = _out_call(
        x2, p[0].reshape(n, -1), yssm, xs, gs, o, gm, row2(ln_emb_g), row2(ln_emb_b),
        row2(d_skip[0]), w_glu[0].astype(BF16), row2(b_glu[0]), w_out[0].astype(BF16),
        w_pg[0].astype(BF16), w_pp[0].astype(BF16), row2(ln_g[0]), row2(ln_b[0]), tm=512)
    return out.reshape(bsz, seq, d)
```

```python
import functools
import math

import numpy as np
import jax
import jax.numpy as jnp
from jax import lax
from jax.experimental import pallas as pl
from jax.experimental.pallas import tpu as pltpu

F32 = jnp.float32
BF16 = jnp.bfloat16

SSM_GROUP = 16
SSM_STATE = 64
MLA_HEADS = 4
QK_NOPE = 128
QK_ROPE = 64
V_HEAD = 128
Q_LORA = 256
KV_LORA = 128
ROPE_THETA = 10000.0
LN_EPS = 1e-5
RMS_EPS = 1e-6
DEPTH = 1
DEEPNORM_ALPHA = (2 * DEPTH) ** 0.25

LANES = 128
SUBLANES = 8
HEAD_PAD = 2 * LANES
S5_CHUNK = 16
GROUPS_PER_TILE = LANES // SSM_GROUP
VMEM_LIMIT = 56 * 1024 * 1024
ROW_TILE_STEPS = 64
ATTN_TILE = 256
S5_GROUP_BLOCK = 4


def _layer_norm(x, g, b):
    mu = jnp.mean(x, axis=-1, keepdims=True)
    xc = x - mu
    var = jnp.mean(xc * xc, axis=-1, keepdims=True)
    return xc * lax.rsqrt(var + LN_EPS) * g + b


def _rms_norm(x, g):
    return x * lax.rsqrt(jnp.mean(x * x, axis=-1, keepdims=True) + RMS_EPS) * g


def _rotate_pair_tile(t, c2, lane):
    p = t * c2
    return jnp.where(lane < QK_ROPE, p + pltpu.roll(p, QK_ROPE, axis=1), 0.0)


def _lane_group_masks():
    grp = lax.broadcasted_iota(jnp.int32, (SUBLANES, LANES), 1) // SSM_GROUP
    return [grp == k for k in range(GROUPS_PER_TILE)]


def _rows_to_chunk_lanes(xs_scr, u_ref, tl):
    masks = _lane_group_masks()
    ng = GROUPS_PER_TILE
    for clp in range(tl // (2 * S5_CHUNK)):
        for hv in range(S5_CHUNK // ng):
            for j in range(xs_scr.shape[0]):
                halves = []
                for cl in (2 * clp, 2 * clp + 1):
                    a = [xs_scr[j, pl.ds(cl * S5_CHUNK + hv * ng + s8, SUBLANES, stride=tl), :]
                         for s8 in range(ng)]
                    tiles = []
                    for gl in range(ng):
                        t = None
                        for s8 in range(ng):
                            rot = (s8 - gl) % ng
                            r = a[s8] if rot == 0 else pltpu.roll(a[s8], rot * SSM_GROUP, axis=1)
                            t = r if t is None else jnp.where(masks[s8], r, t)
                        tiles.append(t)
                    halves.append(tiles)
                for gl in range(ng):
                    blk = jnp.concatenate([halves[0][gl], halves[1][gl]], axis=0).astype(BF16)
                    u_ref[j * ng + gl, clp * 2 * SUBLANES:(clp + 1) * 2 * SUBLANES,
                          hv * LANES:(hv + 1) * LANES] = blk


def _in_proj_kernel(x_ref, pos_ref, invf_ref, lng_ref, lnb_ref, win_ref, qg_ref, wq_ref,
                    kvg_ref, wkv_ref, xs_ref, u_ref, gs_ref, gm_ref, q_ref, k_ref, v_ref,
                    xs_scr, *, q_scale):
    nb, tl, d = x_ref.shape
    tm = nb * tl
    sw = xs_ref.shape[-1]
    xn = _layer_norm(x_ref[...].reshape(tm, d), lng_ref[...], lnb_ref[...]).astype(BF16)
    xs = jnp.dot(xn, win_ref[:, 0:sw], preferred_element_type=F32)
    for j in range(sw // LANES):
        xs_scr[j] = xs[:, j * LANES:(j + 1) * LANES]
    xs_ref[...] = xs.astype(BF16).reshape(nb, tl, sw)
    gs_ref[...] = jnp.dot(xn, win_ref[:, sw:2 * sw],
                          preferred_element_type=F32).astype(BF16).reshape(nb, tl, sw)
    gm_ref[...] = jnp.dot(xn, win_ref[:, 2 * sw:3 * sw],
                          preferred_element_type=F32).astype(BF16).reshape(nb, tl, sw)
    lat = jnp.dot(xn, win_ref[:, 3 * sw:], preferred_element_type=F32)
    cq = lat[:, 0:Q_LORA]
    ckv = lat[:, Q_LORA:Q_LORA + KV_LORA]
    krt = lat[:, Q_LORA + KV_LORA:]

    ang = pos_ref[...].reshape(tm, LANES).astype(F32) * invf_ref[...]
    lane = lax.broadcasted_iota(jnp.int32, ang.shape, 1)
    sn = jnp.sin(ang)
    c2 = jnp.where(lane < QK_ROPE, jnp.cos(ang), jnp.where(lane < QK_ROPE + QK_ROPE // 2, -sn, sn))

    cqn = _rms_norm(cq, qg_ref[...] * q_scale).astype(BF16)
    qf = jnp.dot(cqn, wq_ref[...], preferred_element_type=F32)
    for h in range(MLA_HEADS):
        o = h * HEAD_PAD
        q_ref[:, :, o:o + QK_NOPE] = qf[:, o:o + QK_NOPE].astype(BF16).reshape(nb, tl, QK_NOPE)
        q_ref[:, :, o + QK_NOPE:o + HEAD_PAD] = _rotate_pair_tile(
            qf[:, o + QK_NOPE:o + HEAD_PAD], c2, lane).astype(BF16).reshape(nb, tl, LANES)

    ckvn = _rms_norm(ckv, kvg_ref[...]).astype(BF16)
    kvf = jnp.dot(ckvn, wkv_ref[...], preferred_element_type=F32)
    k_rope = _rotate_pair_tile(krt, c2, lane).astype(BF16).reshape(nb, tl, LANES)
    nv = MLA_HEADS * QK_NOPE
    ones = jnp.ones((nb, tl, LANES), BF16)
    for h in range(MLA_HEADS):
        o = h * HEAD_PAD
        k_ref[:, :, o:o + QK_NOPE] = kvf[:, h * QK_NOPE:(h + 1) * QK_NOPE].astype(BF16).reshape(
            nb, tl, QK_NOPE)
        k_ref[:, :, o + QK_NOPE:o + HEAD_PAD] = k_rope
        v_ref[:, :, o:o + V_HEAD] = kvf[:, nv + h * V_HEAD:nv + (h + 1) * V_HEAD].astype(
            BF16).reshape(nb, tl, V_HEAD)
        v_ref[:, :, o + V_HEAD:o + HEAD_PAD] = ones

    _rows_to_chunk_lanes(xs_scr, u_ref, tl)


def _in_proj_call(x, pos_b, invf, lng, lnb, win, qg, wq, kvg, wkv, *, tl):
    bsz, seq, d = x.shape
    sw = (win.shape[1] - (Q_LORA + KV_LORA + 2 * QK_ROPE)) // 3
    groups = sw // SSM_GROUP
    row = lambda w: pl.BlockSpec((bsz, tl, w), lambda i: (0, i, 0))
    full = lambda a: pl.BlockSpec(a.shape, lambda i: (0,) * a.ndim)
    qk_w = MLA_HEADS * HEAD_PAD
    u_rows = tl // S5_CHUNK * bsz
    act = lambda w: jax.ShapeDtypeStruct((bsz, seq, w), BF16)
    u_shape = jax.ShapeDtypeStruct((groups, seq // S5_CHUNK * bsz, S5_CHUNK * SSM_GROUP), BF16)
    return pl.pallas_call(
        functools.partial(_in_proj_kernel,
                          q_scale=(QK_NOPE + QK_ROPE) ** -0.5 * math.log2(math.e)),
        grid=(seq // tl,),
        in_specs=[row(d), row(LANES), full(invf), full(lng), full(lnb), full(win), full(qg),
                  full(wq), full(kvg), full(wkv)],
        out_specs=(row(sw), pl.BlockSpec((groups, u_rows, S5_CHUNK * SSM_GROUP), lambda i: (0, i, 0)),
                   row(sw), row(sw), row(qk_w), row(qk_w), row(qk_w)),
        out_shape=(act(sw), u_shape, act(sw), act(sw), act(qk_w), act(qk_w), act(qk_w)),
        scratch_shapes=[pltpu.VMEM((sw // LANES, bsz * tl, LANES), F32)],
        compiler_params=pltpu.CompilerParams(dimension_semantics=("arbitrary",),
                                             vmem_limit_bytes=VMEM_LIMIT),
        name="in_proj",
    )(x, pos_b, invf, lng, lnb, win, qg, wq, kvg, wkv)


def _s5_kernel(u_ref, t_ref, wst_ref, wo_ref, ar_ref, ai_ref, ais_ref, y_ref,
               s_ref, ssw_ref, hp_ref, *, gb, n_chunks, bsz):
    for g in range(gb):
        s = jnp.dot(u_ref[g], wst_ref[g], preferred_element_type=F32)
        s_ref[g] = s
        ssw_ref[g] = pltpu.roll(s, SSM_STATE, axis=1)

    ar = [ar_ref[g] for g in range(gb)]
    ai = [ai_ref[g] for g in range(gb)]
    ais = [ais_ref[g] for g in range(gb)]

    def step(c, carry):
        r = pl.multiple_of(c * bsz, bsz)
        new = []
        for g in range(gb):
            h, hsw = carry[2 * g], carry[2 * g + 1]
            hp_ref[g, pl.ds(r, bsz), :] = h
            new.append(ar[g] * h + ai[g] * hsw + s_ref[g, pl.ds(r, bsz), :])
            new.append(ar[g] * hsw + ais[g] * h + ssw_ref[g, pl.ds(r, bsz), :])
        return tuple(new)

    zero = jnp.zeros((bsz, 2 * SSM_STATE), F32)
    lax.fori_loop(0, n_chunks, step, (zero,) * (2 * gb))

    for g in range(gb):
        y = jnp.dot(u_ref[g], t_ref[g], preferred_element_type=F32)
        y = y + jnp.dot(hp_ref[g].astype(BF16), wo_ref[g], preferred_element_type=F32)
        y_ref[g] = y.astype(BF16)


def _s5_call(u, tmat, wst, wo, ar2, ai2, ai2s, *, bsz, gb):
    groups, rows, kw = u.shape
    n_chunks = rows // bsz
    blk = lambda a: pl.BlockSpec((gb,) + a.shape[1:], lambda i: (i,) + (0,) * (a.ndim - 1))
    return pl.pallas_call(
        functools.partial(_s5_kernel, gb=gb, n_chunks=n_chunks, bsz=bsz),
        grid=(groups // gb,),
        in_specs=[blk(u), blk(tmat), blk(wst), blk(wo), blk(ar2), blk(ai2), blk(ai2s)],
        out_specs=blk(u),
        out_shape=jax.ShapeDtypeStruct(u.shape, BF16),
        scratch_shapes=[pltpu.VMEM((gb, rows, 2 * SSM_STATE), F32),
                        pltpu.VMEM((gb, rows, 2 * SSM_STATE), F32),
                        pltpu.VMEM((gb, rows, 2 * SSM_STATE), F32)],
        compiler_params=pltpu.CompilerParams(dimension_semantics=("arbitrary",),
                                             vmem_limit_bytes=VMEM_LIMIT),
        name="s5",
    )(u, tmat, wst, wo, ar2, ai2, ai2s)


def _s5_matrices(a_re, a_im, log_dt, b_re, b_im, c_re, c_im):
    hi = lax.Precision.HIGHEST
    tc = S5_CHUNK
    dt = jnp.exp(log_dt)[:, None]
    mag = jnp.exp(dt * a_re)
    ab_re = mag * jnp.cos(dt * a_im)
    ab_im = mag * jnp.sin(dt * a_im)
    den = a_re * a_re + a_im * a_im
    nr = ab_re - 1.0
    k_re = (nr * a_re + ab_im * a_im) / den
    k_im = (ab_im * a_re - nr * a_im) / den
    taus = jnp.arange(tc + 1, dtype=F32)[:, None, None]
    pmag = jnp.exp(taus * (dt * a_re))
    pw_re = pmag * jnp.cos(taus * (dt * a_im))
    pw_im = pmag * jnp.sin(taus * (dt * a_im))
    bk_re = k_re[..., None] * b_re - k_im[..., None] * b_im
    bk_im = k_re[..., None] * b_im + k_im[..., None] * b_re
    w_re = pw_re[..., None] * bk_re - pw_im[..., None] * bk_im
    w_im = pw_re[..., None] * bk_im + pw_im[..., None] * bk_re
    m = (jnp.einsum('gop,tgpi->tgoi', c_re, w_re[:tc], precision=hi)
         - jnp.einsum('gop,tgpi->tgoi', c_im, w_im[:tc], precision=hi))
    s_idx = jnp.arange(tc)[:, None]
    t_idx = jnp.arange(tc)[None, :]
    lag = t_idx - s_idx
    mt = jnp.where((lag >= 0)[:, :, None, None, None], m[jnp.clip(lag, 0, tc - 1)], 0.0)
    g = a_re.shape[0]
    tmat = mt.transpose(2, 0, 4, 1, 3).reshape(g, tc * SSM_GROUP, tc * SSM_GROUP)
    wr = w_re[tc - 1 - jnp.arange(tc)]
    wi = w_im[tc - 1 - jnp.arange(tc)]
    wst = jnp.concatenate([wr.transpose(1, 0, 3, 2), wi.transpose(1, 0, 3, 2)], axis=-1)
    wst = wst.reshape(g, tc * SSM_GROUP, 2 * SSM_STATE)
    e_re = c_re[None] * pw_re[1:, :, None, :] - c_im[None] * pw_im[1:, :, None, :]
    e_im = c_re[None] * pw_im[1:, :, None, :] + c_im[None] * pw_re[1:, :, None, :]
    wo = jnp.concatenate([e_re, -e_im], axis=-1)
    wo = wo.transpose(1, 3, 0, 2).reshape(g, 2 * SSM_STATE, tc * SSM_GROUP)
    ar, ai = pw_re[tc], pw_im[tc]
    ar2 = jnp.concatenate([ar, ar], axis=-1)[:, None, :]
    ai2 = jnp.concatenate([-ai, ai], axis=-1)[:, None, :]
    return tmat.astype(BF16), wst.astype(BF16), wo.astype(BF16), ar2, ai2, -ai2


def _attn_kernel(q_ref, k_ref, v_ref, o_ref, *, tq):
    i = pl.program_id(1)

    def block(j, carry, masked):
        r = pl.multiple_of(j * tq, tq)
        new = []
        for h in range(MLA_HEADS):
            m, acc = carry[2 * h], carry[2 * h + 1]
            cols = slice(h * HEAD_PAD, (h + 1) * HEAD_PAD)
            s = lax.dot_general(q_ref[:, cols], k_ref[pl.ds(r, tq), cols],
                                (((1,), (1,)), ((), ())), preferred_element_type=F32)
            if masked:
                qi = lax.broadcasted_iota(jnp.int32, s.shape, 0)
                ki = lax.broadcasted_iota(jnp.int32, s.shape, 1)
                s = jnp.where(ki <= qi, s, -jnp.inf)
            m_new = jnp.maximum(m, jnp.max(s, axis=-1, keepdims=True))
            alpha = jnp.exp2(m - m_new)
            p = jnp.exp2(s - jnp.concatenate([m_new] * (tq // LANES), axis=1))
            pv = jnp.dot(p.astype(BF16), v_ref[pl.ds(r, tq), cols], preferred_element_type=F32)
            new += [m_new, jnp.concatenate([alpha, alpha], axis=1) * acc + pv]
        return tuple(new)

    init = (jnp.full((tq, LANES), -jnp.inf, F32), jnp.zeros((tq, HEAD_PAD), F32)) * MLA_HEADS
    carry = lax.fori_loop(0, i, lambda j, c: block(j, c, False), init)
    carry = block(i, carry, True)
    for h in range(MLA_HEADS):
        acc = carry[2 * h + 1]
        o_ref[:, h * V_HEAD:(h + 1) * V_HEAD] = (acc[:, :V_HEAD] / acc[:, V_HEAD:]).astype(BF16)


def _attn_call(q, k, v, *, tq):
    bsz, seq, w = q.shape
    return pl.pallas_call(
        functools.partial(_attn_kernel, tq=tq),
        grid=(bsz, seq // tq),
        in_specs=[pl.BlockSpec((None, tq, w), lambda b, i: (b, i, 0)),
                  pl.BlockSpec((None, seq, w), lambda b, i: (b, 0, 0)),
                  pl.BlockSpec((None, seq, w), lambda b, i: (b, 0, 0))],
        out_specs=pl.BlockSpec((None, tq, MLA_HEADS * V_HEAD), lambda b, i: (b, i, 0)),
        out_shape=jax.ShapeDtypeStruct((bsz, seq, MLA_HEADS * V_HEAD), BF16),
        compiler_params=pltpu.CompilerParams(dimension_semantics=("arbitrary", "arbitrary"),
                                             vmem_limit_bytes=VMEM_LIMIT),
        name="attn",
    )(q, k, v)


def _gelu_tanh(x):
    return 0.5 * x * (1.0 + jnp.tanh(math.sqrt(2.0 / math.pi) * (x + 0.044715 * (x * x * x))))


def _chunk_lanes_to_rows(y_ref, y_scr, nb, tl):
    masks = _lane_group_masks()
    ng = GROUPS_PER_TILE
    n_tiles = y_ref.shape[0] // ng
    for j in range(n_tiles):
        yg = [y_ref[j * ng + gl].astype(F32) for gl in range(ng)]
        for cl in range(tl // S5_CHUNK):
            for hv in range(S5_CHUNK // ng):
                src = [yg[gl][cl * SUBLANES:(cl + 1) * SUBLANES, hv * LANES:(hv + 1) * LANES]
                       for gl in range(ng)]
                for s8 in range(ng):
                    t = None
                    for gl in range(ng):
                        rot = (gl - s8) % ng
                        r = src[gl] if rot == 0 else pltpu.roll(src[gl], rot * SSM_GROUP, axis=1)
                        t = r if t is None else jnp.where(masks[gl], r, t)
                    step = cl * S5_CHUNK + hv * ng + s8
                    y_scr[j, step * nb:(step + 1) * nb, :] = t
    return jnp.concatenate(
        [jnp.concatenate([y_scr[j, pl.ds(b, tl, stride=nb), :] for b in range(nb)], axis=0)
         for j in range(n_tiles)], axis=1)


def _out_kernel(x_ref, p_ref, yssm_ref, xs_ref, gs_ref, o_ref, gm_ref, lng0_ref, lnb0_ref,
                dskip_ref, wglu_ref, bglu_ref, wout_ref, wpg_ref, wpp_ref, lng_ref, lnb_ref,
                out_ref, y_scr):
    nb, tl, d = x_ref.shape
    tm = nb * tl
    sw = xs_ref.shape[-1]
    flat = lambda ref: ref[...].reshape(tm, ref.shape[-1])
    xn = _layer_norm(flat(x_ref), lng0_ref[...], lnb0_ref[...])
    yssm = _chunk_lanes_to_rows(yssm_ref, y_scr, nb, tl)
    y = yssm + dskip_ref[...] * flat(xs_ref).astype(F32)
    y = _gelu_tanh(y)
    glu = jnp.dot(y.astype(BF16), wglu_ref[...], preferred_element_type=F32) + bglu_ref[...]
    gs = flat(gs_ref).astype(F32)
    ys = y * jax.nn.sigmoid(glu) * (gs * jax.nn.sigmoid(gs))
    gm = flat(gm_ref).astype(F32)
    ym = flat(o_ref).astype(F32) * (gm * jax.nn.sigmoid(gm))
    mix = (jnp.dot(ys.astype(BF16), wout_ref[0:sw, :], preferred_element_type=F32)
           + jnp.dot(ym.astype(BF16), wout_ref[sw:, :], preferred_element_type=F32))
    u = DEEPNORM_ALPHA * xn + mix
    gate = jax.nn.sigmoid(jnp.dot(u.astype(BF16), wpg_ref[...], preferred_element_type=F32))
    ple = gate * jnp.dot(flat(p_ref).astype(BF16), wpp_ref[...], preferred_element_type=F32)
    out_ref[...] = _layer_norm(u + ple, lng_ref[...], lnb_ref[...]).reshape(nb, tl, d)


def _out_call(x, p3, yg, xs, gs, o, gm, lng0, lnb0, dskip, wglu, bglu, wout, wpg, wpp,
              lng, lnb, *, tl):
    bsz, seq, d = x.shape
    row = lambda a: pl.BlockSpec((bsz, tl, a.shape[-1]), lambda i: (0, i, 0))
    full = lambda a: pl.BlockSpec(a.shape, lambda i: (0,) * a.ndim)
    y_spec = pl.BlockSpec((yg.shape[0], tl // S5_CHUNK * bsz, yg.shape[2]), lambda i: (0, i, 0))
    params = (lng0, lnb0, dskip, wglu, bglu, wout, wpg, wpp, lng, lnb)
    return pl.pallas_call(
        _out_kernel,
        grid=(seq // tl,),
        in_specs=[row(x), row(p3), y_spec, row(xs), row(gs), row(o), row(gm)]
                 + [full(a) for a in params],
        out_specs=pl.BlockSpec((bsz, tl, d), lambda i: (0, i, 0)),
        out_shape=jax.ShapeDtypeStruct((bsz, seq, d), F32),
        scratch_shapes=[pltpu.VMEM((xs.shape[-1] // LANES, bsz * tl, LANES), F32)],
        compiler_params=pltpu.CompilerParams(dimension_semantics=("arbitrary",),
                                             vmem_limit_bytes=VMEM_LIMIT),
        name="out_proj",
    )(x, p3, yg, xs, gs, o, gm, *params)


def _swap_halves_cols(w):
    half = w.shape[-1] // 2
    return jnp.concatenate([w[..., half:], w[..., :half]], axis=-1)


def kernel(x, p, positions, ln_emb_g, ln_emb_b, w_in, a_re, a_im, log_dt, b_re, b_im, c_re,
           c_im, d_skip, w_glu, b_glu, q_norm_g, w_uq, kv_norm_g, w_ukv, w_out, w_pg, w_pp,
           ln_g, ln_b):
    bsz, seq, d = x.shape
    assert bsz == SUBLANES, "the S5 recurrence keeps the batch on the 8 sublanes of a vreg"
    sw = d_skip.shape[-1]
    mw = MLA_HEADS * V_HEAD
    row2 = lambda a: a.reshape(1, -1)

    wi = w_in[0]
    o_cq = 2 * sw
    o_ckv = o_cq + Q_LORA
    o_kr = o_ckv + KV_LORA
    o_gm = o_kr + QK_ROPE
    kr_cols = wi[:, o_kr:o_gm]
    win = jnp.concatenate([wi[:, 0:sw], wi[:, sw:2 * sw], wi[:, o_gm:o_gm + mw],
                           wi[:, o_cq:o_ckv], wi[:, o_ckv:o_kr], kr_cols,
                           _swap_halves_cols(kr_cols)], axis=1).astype(BF16)
    wq3 = w_uq[0].reshape(Q_LORA, MLA_HEADS, QK_NOPE + QK_ROPE)
    wq = jnp.concatenate([wq3, _swap_halves_cols(wq3[..., QK_NOPE:])], axis=-1)
    wq = wq.reshape(Q_LORA, MLA_HEADS * HEAD_PAD).astype(BF16)
    wkv3 = w_ukv[0].reshape(KV_LORA, MLA_HEADS, QK_NOPE + V_HEAD)
    wkv = jnp.concatenate([wkv3[..., :QK_NOPE].reshape(KV_LORA, -1),
                           wkv3[..., QK_NOPE:].reshape(KV_LORA, -1)], axis=1).astype(BF16)

    inv_freq = 1.0 / (ROPE_THETA ** (np.arange(0, QK_ROPE, 2, dtype=np.float32) / QK_ROPE))
    invf = jnp.asarray(np.tile(inv_freq, LANES // inv_freq.size).reshape(1, LANES), F32)
    pos_b = jnp.broadcast_to(positions[:, :, None], (bsz, seq, LANES))

    xs, u, gs, gm, q, k, v = _in_proj_call(
        x, pos_b, invf, row2(ln_emb_g), row2(ln_emb_b), win, row2(q_norm_g[0]), wq,
        row2(kv_norm_g[0]), wkv, tl=ROW_TILE_STEPS)

    tmat, wst, wo, ar2, ai2, ai2s = _s5_matrices(a_re[0], a_im[0], log_dt[0], b_re[0], b_im[0],
                                                 c_re[0], c_im[0])
    yg = _s5_call(u, tmat, wst, wo, ar2, ai2, ai2s, bsz=bsz, gb=S5_GROUP_BLOCK)

    o = _attn_call(q, k, v, tq=ATTN_TILE)

    return _out_call(
        x, p[0], yg, xs, gs, o, gm, row2(ln_emb_g), row2(ln_emb_b),
        row2(d_skip[0]), w_glu[0].astype(BF16), row2(b_glu[0]), w_out[0].astype(BF16),
        w_pg[0].astype(BF16), w_pp[0].astype(BF16), row2(ln_g[0]), row2(ln_b[0]),
        tl=ROW_TILE_STEPS)
```

```python
import functools
import math

import numpy as np
import jax
import jax.numpy as jnp
from jax import lax
from jax.experimental import pallas as pl
from jax.experimental.pallas import tpu as pltpu

F32 = jnp.float32
BF16 = jnp.bfloat16

SSM_GROUP = 16
SSM_STATE = 64
MLA_HEADS = 4
QK_NOPE = 128
QK_ROPE = 64
V_HEAD = 128
Q_LORA = 256
KV_LORA = 128
ROPE_THETA = 10000.0
LN_EPS = 1e-5
RMS_EPS = 1e-6
DEPTH = 1
DEEPNORM_ALPHA = (2 * DEPTH) ** 0.25

LANES = 128
SUBLANES = 8
HEAD_PAD = 2 * LANES
VT_ROWS = V_HEAD + 16
S5_CHUNK = 16
GROUPS_PER_TILE = LANES // SSM_GROUP
VMEM_LIMIT = 56 * 1024 * 1024
ROW_TILE_STEPS = 64
ATTN_TILE = 256
S5_GROUP_BLOCK = 4


def _layer_norm(x, g, b):
    mu = jnp.mean(x, axis=-1, keepdims=True)
    xc = x - mu
    var = jnp.mean(xc * xc, axis=-1, keepdims=True)
    return xc * lax.rsqrt(var + LN_EPS) * g + b


def _rms_norm(x, g):
    return x * lax.rsqrt(jnp.mean(x * x, axis=-1, keepdims=True) + RMS_EPS) * g


def _rotate_pair_tile(t, c2, lane):
    p = t * c2
    return jnp.where(lane < QK_ROPE, p + pltpu.roll(p, QK_ROPE, axis=1), 0.0)


def _lane_group_masks():
    grp = lax.broadcasted_iota(jnp.int32, (SUBLANES, LANES), 1) // SSM_GROUP
    return [grp == k for k in range(GROUPS_PER_TILE)]


def _rows_to_chunk_lanes(xs_scr, u_ref, tl):
    masks = _lane_group_masks()
    ng = GROUPS_PER_TILE
    for clp in range(tl // (2 * S5_CHUNK)):
        for hv in range(S5_CHUNK // ng):
            for j in range(xs_scr.shape[0]):
                halves = []
                for cl in (2 * clp, 2 * clp + 1):
                    a = [xs_scr[j, pl.ds(cl * S5_CHUNK + hv * ng + s8, SUBLANES, stride=tl), :]
                         for s8 in range(ng)]
                    tiles = []
                    for gl in range(ng):
                        t = None
                        for s8 in range(ng):
                            rot = (s8 - gl) % ng
                            r = a[s8] if rot == 0 else pltpu.roll(a[s8], rot * SSM_GROUP, axis=1)
                            t = r if t is None else jnp.where(masks[s8], r, t)
                        tiles.append(t)
                    halves.append(tiles)
                for gl in range(ng):
                    blk = jnp.concatenate([halves[0][gl], halves[1][gl]], axis=0).astype(BF16)
                    u_ref[j * ng + gl, clp * 2 * SUBLANES:(clp + 1) * 2 * SUBLANES,
                          hv * LANES:(hv + 1) * LANES] = blk


def _in_proj_kernel(x_ref, pos_ref, invf_ref, lng_ref, lnb_ref, win_ref, qg_ref, wq_ref,
                    kvg_ref, wkv_ref, xs_ref, u_ref, gs_ref, gm_ref, q_ref, k_ref, v_ref,
                    xs_scr, *, q_scale):
    nb, tl, d = x_ref.shape
    tm = nb * tl
    sw = xs_ref.shape[-1]
    xn = _layer_norm(x_ref[...].reshape(tm, d), lng_ref[...], lnb_ref[...]).astype(BF16)
    xs = jnp.dot(xn, win_ref[:, 0:sw], preferred_element_type=F32)
    for j in range(sw // LANES):
        xs_scr[j] = xs[:, j * LANES:(j + 1) * LANES]
    xs_ref[...] = xs.astype(BF16).reshape(nb, tl, sw)
    gs_ref[...] = jnp.dot(xn, win_ref[:, sw:2 * sw],
                          preferred_element_type=F32).astype(BF16).reshape(nb, tl, sw)
    gm_ref[...] = jnp.dot(xn, win_ref[:, 2 * sw:3 * sw],
                          preferred_element_type=F32).astype(BF16).reshape(nb, tl, sw)
    lat = jnp.dot(xn, win_ref[:, 3 * sw:], preferred_element_type=F32)
    cq = lat[:, 0:Q_LORA]
    ckv = lat[:, Q_LORA:Q_LORA + KV_LORA]
    krt = lat[:, Q_LORA + KV_LORA:]

    ang = pos_ref[...].reshape(tm, LANES).astype(F32) * invf_ref[...]
    lane = lax.broadcasted_iota(jnp.int32, ang.shape, 1)
    sn = jnp.sin(ang)
    c2 = jnp.where(lane < QK_ROPE, jnp.cos(ang), jnp.where(lane < QK_ROPE + QK_ROPE // 2, -sn, sn))

    cqn = _rms_norm(cq, qg_ref[...] * q_scale).astype(BF16)
    qf = jnp.dot(cqn, wq_ref[...], preferred_element_type=F32)
    for h in range(MLA_HEADS):
        o = h * HEAD_PAD
        q_ref[:, :, o:o + QK_NOPE] = qf[:, o:o + QK_NOPE].astype(BF16).reshape(nb, tl, QK_NOPE)
        q_ref[:, :, o + QK_NOPE:o + HEAD_PAD] = _rotate_pair_tile(
            qf[:, o + QK_NOPE:o + HEAD_PAD], c2, lane).astype(BF16).reshape(nb, tl, LANES)

    ckvn = _rms_norm(ckv, kvg_ref[...]).astype(BF16)
    kvf = jnp.dot(ckvn, wkv_ref[...], preferred_element_type=F32)
    k_rope = _rotate_pair_tile(krt, c2, lane).astype(BF16).reshape(nb, tl, LANES)
    nv = MLA_HEADS * QK_NOPE
    v_ref[...] = kvf[:, nv:].astype(BF16).reshape(nb, tl, MLA_HEADS * V_HEAD)
    for h in range(MLA_HEADS):
        o = h * HEAD_PAD
        k_ref[:, :, o:o + QK_NOPE] = kvf[:, h * QK_NOPE:(h + 1) * QK_NOPE].astype(BF16).reshape(
            nb, tl, QK_NOPE)
        k_ref[:, :, o + QK_NOPE:o + HEAD_PAD] = k_rope

    _rows_to_chunk_lanes(xs_scr, u_ref, tl)


def _in_proj_call(x, pos_b, invf, lng, lnb, win, qg, wq, kvg, wkv, *, tl):
    bsz, seq, d = x.shape
    sw = (win.shape[1] - (Q_LORA + KV_LORA + 2 * QK_ROPE)) // 3
    groups = sw // SSM_GROUP
    row = lambda w: pl.BlockSpec((bsz, tl, w), lambda i: (0, i, 0))
    full = lambda a: pl.BlockSpec(a.shape, lambda i: (0,) * a.ndim)
    qk_w = MLA_HEADS * HEAD_PAD
    u_rows = tl // S5_CHUNK * bsz
    act = lambda w: jax.ShapeDtypeStruct((bsz, seq, w), BF16)
    u_shape = jax.ShapeDtypeStruct((groups, seq // S5_CHUNK * bsz, S5_CHUNK * SSM_GROUP), BF16)
    return pl.pallas_call(
        functools.partial(_in_proj_kernel,
                          q_scale=(QK_NOPE + QK_ROPE) ** -0.5 * math.log2(math.e)),
        grid=(seq // tl,),
        in_specs=[row(d), row(LANES), full(invf), full(lng), full(lnb), full(win), full(qg),
                  full(wq), full(kvg), full(wkv)],
        out_specs=(row(sw), pl.BlockSpec((groups, u_rows, S5_CHUNK * SSM_GROUP), lambda i: (0, i, 0)),
                   row(sw), row(sw), row(qk_w), row(qk_w), row(MLA_HEADS * V_HEAD)),
        out_shape=(act(sw), u_shape, act(sw), act(sw), act(qk_w), act(qk_w),
                   act(MLA_HEADS * V_HEAD)),
        scratch_shapes=[pltpu.VMEM((sw // LANES, bsz * tl, LANES), F32)],
        compiler_params=pltpu.CompilerParams(dimension_semantics=("arbitrary",),
                                             vmem_limit_bytes=VMEM_LIMIT),
        name="in_proj",
    )(x, pos_b, invf, lng, lnb, win, qg, wq, kvg, wkv)


def _s5_kernel(u_ref, t_ref, wst_ref, wo_ref, ar_ref, ai_ref, ais_ref, y_ref,
               s_ref, ssw_ref, hp_ref, *, gb, n_chunks, bsz):
    for g in range(gb):
        s = jnp.dot(u_ref[g], wst_ref[g], preferred_element_type=F32)
        s_ref[g] = s
        ssw_ref[g] = pltpu.roll(s, SSM_STATE, axis=1)

    ar = [ar_ref[g] for g in range(gb)]
    ai = [ai_ref[g] for g in range(gb)]
    ais = [ais_ref[g] for g in range(gb)]

    def step(c, carry):
        r = pl.multiple_of(c * bsz, bsz)
        new = []
        for g in range(gb):
            h, hsw = carry[2 * g], carry[2 * g + 1]
            hp_ref[g, pl.ds(r, bsz), :] = h
            new.append(ar[g] * h + ai[g] * hsw + s_ref[g, pl.ds(r, bsz), :])
            new.append(ar[g] * hsw + ais[g] * h + ssw_ref[g, pl.ds(r, bsz), :])
        return tuple(new)

    zero = jnp.zeros((bsz, 2 * SSM_STATE), F32)
    lax.fori_loop(0, n_chunks, step, (zero,) * (2 * gb))

    for g in range(gb):
        y = jnp.dot(u_ref[g], t_ref[g], preferred_element_type=F32)
        y = y + jnp.dot(hp_ref[g].astype(BF16), wo_ref[g], preferred_element_type=F32)
        y_ref[g] = y.astype(BF16)


def _s5_call(u, tmat, wst, wo, ar2, ai2, ai2s, *, bsz, gb):
    groups, rows, kw = u.shape
    n_chunks = rows // bsz
    blk = lambda a: pl.BlockSpec((gb,) + a.shape[1:], lambda i: (i,) + (0,) * (a.ndim - 1))
    return pl.pallas_call(
        functools.partial(_s5_kernel, gb=gb, n_chunks=n_chunks, bsz=bsz),
        grid=(groups // gb,),
        in_specs=[blk(u), blk(tmat), blk(wst), blk(wo), blk(ar2), blk(ai2), blk(ai2s)],
        out_specs=blk(u),
        out_shape=jax.ShapeDtypeStruct(u.shape, BF16),
        scratch_shapes=[pltpu.VMEM((gb, rows, 2 * SSM_STATE), F32),
                        pltpu.VMEM((gb, rows, 2 * SSM_STATE), F32),
                        pltpu.VMEM((gb, rows, 2 * SSM_STATE), F32)],
        compiler_params=pltpu.CompilerParams(dimension_semantics=("arbitrary",),
                                             vmem_limit_bytes=VMEM_LIMIT),
        name="s5",
    )(u, tmat, wst, wo, ar2, ai2, ai2s)


def _s5_matrices(a_re, a_im, log_dt, b_re, b_im, c_re, c_im):
    hi = lax.Precision.HIGHEST
    tc = S5_CHUNK
    dt = jnp.exp(log_dt)[:, None]
    mag = jnp.exp(dt * a_re)
    ab_re = mag * jnp.cos(dt * a_im)
    ab_im = mag * jnp.sin(dt * a_im)
    den = a_re * a_re + a_im * a_im
    nr = ab_re - 1.0
    k_re = (nr * a_re + ab_im * a_im) / den
    k_im = (ab_im * a_re - nr * a_im) / den
    taus = jnp.arange(tc + 1, dtype=F32)[:, None, None]
    pmag = jnp.exp(taus * (dt * a_re))
    pw_re = pmag * jnp.cos(taus * (dt * a_im))
    pw_im = pmag * jnp.sin(taus * (dt * a_im))
    bk_re = k_re[..., None] * b_re - k_im[..., None] * b_im
    bk_im = k_re[..., None] * b_im + k_im[..., None] * b_re
    w_re = pw_re[..., None] * bk_re - pw_im[..., None] * bk_im
    w_im = pw_re[..., None] * bk_im + pw_im[..., None] * bk_re
    m = (jnp.einsum('gop,tgpi->tgoi', c_re, w_re[:tc], precision=hi)
         - jnp.einsum('gop,tgpi->tgoi', c_im, w_im[:tc], precision=hi))
    s_idx = jnp.arange(tc)[:, None]
    t_idx = jnp.arange(tc)[None, :]
    lag = t_idx - s_idx
    mt = jnp.where((lag >= 0)[:, :, None, None, None], m[jnp.clip(lag, 0, tc - 1)], 0.0)
    g = a_re.shape[0]
    tmat = mt.transpose(2, 0, 4, 1, 3).reshape(g, tc * SSM_GROUP, tc * SSM_GROUP)
    wr = w_re[tc - 1 - jnp.arange(tc)]
    wi = w_im[tc - 1 - jnp.arange(tc)]
    wst = jnp.concatenate([wr.transpose(1, 0, 3, 2), wi.transpose(1, 0, 3, 2)], axis=-1)
    wst = wst.reshape(g, tc * SSM_GROUP, 2 * SSM_STATE)
    e_re = c_re[None] * pw_re[1:, :, None, :] - c_im[None] * pw_im[1:, :, None, :]
    e_im = c_re[None] * pw_im[1:, :, None, :] + c_im[None] * pw_re[1:, :, None, :]
    wo = jnp.concatenate([e_re, -e_im], axis=-1)
    wo = wo.transpose(1, 3, 0, 2).reshape(g, 2 * SSM_STATE, tc * SSM_GROUP)
    ar, ai = pw_re[tc], pw_im[tc]
    ar2 = jnp.concatenate([ar, ar], axis=-1)[:, None, :]
    ai2 = jnp.concatenate([-ai, ai], axis=-1)[:, None, :]
    return tmat.astype(BF16), wst.astype(BF16), wo.astype(BF16), ar2, ai2, -ai2


def _attn_kernel(q_ref, k_ref, v_ref, o_ref, vt_scr, st_scr, p_scr, acc_scr, *, tq):
    i = pl.program_id(1)
    n_kb = k_ref.shape[0] // tq

    @pl.when(i == 0)
    def _():
        ones = jnp.ones((VT_ROWS - V_HEAD, tq), BF16)
        for h in range(MLA_HEADS):
            for jb in range(n_kb):
                blk = v_ref[jb * tq:(jb + 1) * tq, h * V_HEAD:(h + 1) * V_HEAD]
                vt_scr[h, jb, 0:V_HEAD, :] = blk.astype(F32).T.astype(BF16)
                vt_scr[h, jb, V_HEAD:VT_ROWS, :] = ones

    heads = range(MLA_HEADS)
    cols = [slice(h * HEAD_PAD, (h + 1) * HEAD_PAD) for h in heads]

    def scores(j, slot, h):
        r = pl.multiple_of(j * tq, tq)
        st_scr[slot, h] = lax.dot_general(k_ref[pl.ds(r, tq), cols[h]], q_ref[:, cols[h]],
                                          (((1,), (1,)), ((), ())), preferred_element_type=F32)

    def softmax(slot, h, m, masked):
        s = st_scr[slot, h]
        if masked:
            ki = lax.broadcasted_iota(jnp.int32, s.shape, 0)
            qi = lax.broadcasted_iota(jnp.int32, s.shape, 1)
            s = jnp.where(ki <= qi, s, -jnp.inf)
        m_new = jnp.maximum(m, jnp.max(s, axis=0, keepdims=True))
        p_scr[slot, h] = jnp.exp2(s - m_new).astype(BF16)
        return jnp.exp2(m - m_new), m_new

    def accumulate(j, slot, h, alpha):
        acc_scr[h] = alpha * acc_scr[h] + jnp.dot(vt_scr[h, j], p_scr[slot, h],
                                                  preferred_element_type=F32)

    def trip(j, slot, carry):
        alpha, m = carry
        new_alpha, new_m = [], []
        for h in heads:
            scores(j + 1, 1 - slot, h)
            accumulate(jnp.maximum(j - 1, 0), 1 - slot, h, alpha[h])
            a, mm = softmax(slot, h, m[h], False)
            new_alpha.append(a)
            new_m.append(mm)
        return new_alpha, new_m

    def finish(slot, carry):
        alpha, m = carry
        for h in heads:
            accumulate(jnp.maximum(i - 1, 0), 1 - slot, h, alpha[h])
            a, _ = softmax(slot, h, m[h], True)
            accumulate(i, slot, h, a)
            acc = acc_scr[h]
            o_t = acc[:V_HEAD] / acc[V_HEAD:V_HEAD + 1]
            o_ref[:, h * V_HEAD:(h + 1) * V_HEAD] = o_t.T.astype(BF16)

    p_scr[1] = jnp.zeros(p_scr.shape[1:], BF16)
    acc_scr[...] = jnp.zeros(acc_scr.shape, F32)
    for h in heads:
        scores(0, 0, h)
    init = ([jnp.ones((1, tq), F32) for _ in heads], [jnp.full((1, tq), -jnp.inf, F32) for _ in heads])
    carry = lax.fori_loop(0, lax.shift_right_logical(i, 1),
                          lambda t, c: trip(2 * t + 1, 1, trip(2 * t, 0, c)), init)

    @pl.when((i & 1) == 0)
    def _():
        finish(0, carry)

    @pl.when((i & 1) == 1)
    def _():
        finish(1, trip(i - 1, 0, carry))


def _attn_call(q, k, v, *, tq):
    bsz, seq, w = q.shape
    return pl.pallas_call(
        functools.partial(_attn_kernel, tq=tq),
        grid=(bsz, seq // tq),
        in_specs=[pl.BlockSpec((None, tq, w), lambda b, i: (b, i, 0)),
                  pl.BlockSpec((None, seq, w), lambda b, i: (b, 0, 0)),
                  pl.BlockSpec((None, seq, v.shape[-1]), lambda b, i: (b, 0, 0))],
        out_specs=pl.BlockSpec((None, tq, MLA_HEADS * V_HEAD), lambda b, i: (b, i, 0)),
        out_shape=jax.ShapeDtypeStruct((bsz, seq, MLA_HEADS * V_HEAD), BF16),
        scratch_shapes=[pltpu.VMEM((MLA_HEADS, seq // tq, VT_ROWS, tq), BF16),
                        pltpu.VMEM((2, MLA_HEADS, tq, tq), F32),
                        pltpu.VMEM((2, MLA_HEADS, tq, tq), BF16),
                        pltpu.VMEM((MLA_HEADS, VT_ROWS, tq), F32)],
        compiler_params=pltpu.CompilerParams(dimension_semantics=("arbitrary", "arbitrary"),
                                             vmem_limit_bytes=VMEM_LIMIT),
        name="attn",
    )(q, k, v)


def _gelu_tanh(x):
    return 0.5 * x * (1.0 + jnp.tanh(math.sqrt(2.0 / math.pi) * (x + 0.044715 * (x * x * x))))


def _chunk_lanes_to_rows(y_ref, y_scr, nb, tl):
    masks = _lane_group_masks()
    ng = GROUPS_PER_TILE
    n_tiles = y_ref.shape[0] // ng
    for j in range(n_tiles):
        yg = [y_ref[j * ng + gl].astype(F32) for gl in range(ng)]
        for cl in range(tl // S5_CHUNK):
            for hv in range(S5_CHUNK // ng):
                src = [yg[gl][cl * SUBLANES:(cl + 1) * SUBLANES, hv * LANES:(hv + 1) * LANES]
                       for gl in range(ng)]
                for s8 in range(ng):
                    t = None
                    for gl in range(ng):
                        rot = (gl - s8) % ng
                        r = src[gl] if rot == 0 else pltpu.roll(src[gl], rot * SSM_GROUP, axis=1)
                        t = r if t is None else jnp.where(masks[gl], r, t)
                    step = cl * S5_CHUNK + hv * ng + s8
                    y_scr[j, step * nb:(step + 1) * nb, :] = t
    return jnp.concatenate(
        [jnp.concatenate([y_scr[j, pl.ds(b, tl, stride=nb), :] for b in range(nb)], axis=0)
         for j in range(n_tiles)], axis=1)


def _out_kernel(x_ref, p_ref, yssm_ref, xs_ref, gs_ref, o_ref, gm_ref, lng0_ref, lnb0_ref,
                dskip_ref, wglu_ref, bglu_ref, wout_ref, wpg_ref, wpp_ref, lng_ref, lnb_ref,
                out_ref, y_scr):
    nb, tl, d = x_ref.shape
    tm = nb * tl
    sw = xs_ref.shape[-1]
    flat = lambda ref: ref[...].reshape(tm, ref.shape[-1])
    xn = _layer_norm(flat(x_ref), lng0_ref[...], lnb0_ref[...])
    yssm = _chunk_lanes_to_rows(yssm_ref, y_scr, nb, tl)
    y = yssm + dskip_ref[...] * flat(xs_ref).astype(F32)
    y = _gelu_tanh(y)
    glu = jnp.dot(y.astype(BF16), wglu_ref[...], preferred_element_type=F32) + bglu_ref[...]
    gs = flat(gs_ref).astype(F32)
    ys = y * jax.nn.sigmoid(glu) * (gs * jax.nn.sigmoid(gs))
    gm = flat(gm_ref).astype(F32)
    ym = flat(o_ref).astype(F32) * (gm * jax.nn.sigmoid(gm))
    mix = (jnp.dot(ys.astype(BF16), wout_ref[0:sw, :], preferred_element_type=F32)
           + jnp.dot(ym.astype(BF16), wout_ref[sw:, :], preferred_element_type=F32))
    u = DEEPNORM_ALPHA * xn + mix
    gate = jax.nn.sigmoid(jnp.dot(u.astype(BF16), wpg_ref[...], preferred_element_type=F32))
    ple = gate * jnp.dot(flat(p_ref).astype(BF16), wpp_ref[...], preferred_element_type=F32)
    out_ref[...] = _layer_norm(u + ple, lng_ref[...], lnb_ref[...]).reshape(nb, tl, d)


def _out_call(x, p3, yg, xs, gs, o, gm, lng0, lnb0, dskip, wglu, bglu, wout, wpg, wpp,
              lng, lnb, *, tl):
    bsz, seq, d = x.shape
    row = lambda a: pl.BlockSpec((bsz, tl, a.shape[-1]), lambda i: (0, i, 0))
    full = lambda a: pl.BlockSpec(a.shape, lambda i: (0,) * a.ndim)
    y_spec = pl.BlockSpec((yg.shape[0], tl // S5_CHUNK * bsz, yg.shape[2]), lambda i: (0, i, 0))
    params = (lng0, lnb0, dskip, wglu, bglu, wout, wpg, wpp, lng, lnb)
    return pl.pallas_call(
        _out_kernel,
        grid=(seq // tl,),
        in_specs=[row(x), row(p3), y_spec, row(xs), row(gs), row(o), row(gm)]
                 + [full(a) for a in params],
        out_specs=pl.BlockSpec((bsz, tl, d), lambda i: (0, i, 0)),
        out_shape=jax.ShapeDtypeStruct((bsz, seq, d), F32),
        scratch_shapes=[pltpu.VMEM((xs.shape[-1] // LANES, bsz * tl, LANES), F32)],
        compiler_params=pltpu.CompilerParams(dimension_semantics=("arbitrary",),
                                             vmem_limit_bytes=VMEM_LIMIT),
        name="out_proj",
    )(x, p3, yg, xs, gs, o, gm, *params)


def _swap_halves_cols(w):
    half = w.shape[-1] // 2
    return jnp.concatenate([w[..., half:], w[..., :half]], axis=-1)


def kernel(x, p, positions, ln_emb_g, ln_emb_b, w_in, a_re, a_im, log_dt, b_re, b_im, c_re,
           c_im, d_skip, w_glu, b_glu, q_norm_g, w_uq, kv_norm_g, w_ukv, w_out, w_pg, w_pp,
           ln_g, ln_b):
    bsz, seq, d = x.shape
    assert bsz == SUBLANES, "the S5 recurrence keeps the batch on the 8 sublanes of a vreg"
    sw = d_skip.shape[-1]
    mw = MLA_HEADS * V_HEAD
    row2 = lambda a: a.reshape(1, -1)

    wi = w_in[0]
    o_cq = 2 * sw
    o_ckv = o_cq + Q_LORA
    o_kr = o_ckv + KV_LORA
    o_gm = o_kr + QK_ROPE
    kr_cols = wi[:, o_kr:o_gm]
    win = jnp.concatenate([wi[:, 0:sw], wi[:, sw:2 * sw], wi[:, o_gm:o_gm + mw],
                           wi[:, o_cq:o_ckv], wi[:, o_ckv:o_kr], kr_cols,
                           _swap_halves_cols(kr_cols)], axis=1).astype(BF16)
    wq3 = w_uq[0].reshape(Q_LORA, MLA_HEADS, QK_NOPE + QK_ROPE)
    wq = jnp.concatenate([wq3, _swap_halves_cols(wq3[..., QK_NOPE:])], axis=-1)
    wq = wq.reshape(Q_LORA, MLA_HEADS * HEAD_PAD).astype(BF16)
    wkv3 = w_ukv[0].reshape(KV_LORA, MLA_HEADS, QK_NOPE + V_HEAD)
    wkv = jnp.concatenate([wkv3[..., :QK_NOPE].reshape(KV_LORA, -1),
                           wkv3[..., QK_NOPE:].reshape(KV_LORA, -1)], axis=1).astype(BF16)

    inv_freq = 1.0 / (ROPE_THETA ** (np.arange(0, QK_ROPE, 2, dtype=np.float32) / QK_ROPE))
    invf = jnp.asarray(np.tile(inv_freq, LANES // inv_freq.size).reshape(1, LANES), F32)
    pos_b = jnp.broadcast_to(positions[:, :, None], (bsz, seq, LANES))

    xs, u, gs, gm, q, k, v = _in_proj_call(
        x, pos_b, invf, row2(ln_emb_g), row2(ln_emb_b), win, row2(q_norm_g[0]), wq,
        row2(kv_norm_g[0]), wkv, tl=ROW_TILE_STEPS)

    tmat, wst, wo, ar2, ai2, ai2s = _s5_matrices(a_re[0], a_im[0], log_dt[0], b_re[0], b_im[0],
                                                 c_re[0], c_im[0])
    yg = _s5_call(u, tmat, wst, wo, ar2, ai2, ai2s, bsz=bsz, gb=S5_GROUP_BLOCK)

    o = _attn_call(q, k, v, tq=ATTN_TILE)

    return _out_call(
        x, p[0], yg, xs, gs, o, gm, row2(ln_emb_g), row2(ln_emb_b),
        row2(d_skip[0]), w_glu[0].astype(BF16), row2(b_glu[0]), w_out[0].astype(BF16),
        w_pg[0].astype(BF16), w_pp[0].astype(BF16), row2(ln_g[0]), row2(ln_b[0]),
        tl=ROW_TILE_STEPS)
```

```python
import functools
import math

import numpy as np
import jax
import jax.numpy as jnp
from jax import lax
from jax.experimental import pallas as pl
from jax.experimental.pallas import tpu as pltpu

F32 = jnp.float32
BF16 = jnp.bfloat16

SSM_GROUP = 16
SSM_STATE = 64
MLA_HEADS = 4
QK_NOPE = 128
QK_ROPE = 64
V_HEAD = 128
Q_LORA = 256
KV_LORA = 128
ROPE_THETA = 10000.0
LN_EPS = 1e-5
RMS_EPS = 1e-6
DEPTH = 1
DEEPNORM_ALPHA = (2 * DEPTH) ** 0.25

LANES = 128
SUBLANES = 8
HEAD_PAD = 2 * LANES
VT_ROWS = V_HEAD + 16
S5_CHUNK = 16
GROUPS_PER_TILE = LANES // SSM_GROUP
VMEM_LIMIT = 56 * 1024 * 1024
ROW_TILE_STEPS = 64
ATTN_TILE = 256
S5_GROUP_BLOCK = 4


def _layer_norm(x, g, b):
    mu = jnp.mean(x, axis=-1, keepdims=True)
    xc = x - mu
    var = jnp.mean(xc * xc, axis=-1, keepdims=True)
    return xc * lax.rsqrt(var + LN_EPS) * g + b


def _rms_norm(x, g):
    return x * lax.rsqrt(jnp.mean(x * x, axis=-1, keepdims=True) + RMS_EPS) * g


def _rotate_pair_tile(t, c2, lane):
    p = t * c2
    return jnp.where(lane < QK_ROPE, p + pltpu.roll(p, QK_ROPE, axis=1), 0.0)


def _lane_group_masks():
    grp = lax.broadcasted_iota(jnp.int32, (SUBLANES, LANES), 1) // SSM_GROUP
    return [grp == k for k in range(GROUPS_PER_TILE)]


def _rows_to_chunk_lanes(xs_scr, u_ref, tl):
    masks = _lane_group_masks()
    ng = GROUPS_PER_TILE
    for clp in range(tl // (2 * S5_CHUNK)):
        for hv in range(S5_CHUNK // ng):
            for j in range(xs_scr.shape[0]):
                halves = []
                for cl in (2 * clp, 2 * clp + 1):
                    a = [xs_scr[j, pl.ds(cl * S5_CHUNK + hv * ng + s8, SUBLANES, stride=tl), :]
                         for s8 in range(ng)]
                    tiles = []
                    for gl in range(ng):
                        t = None
                        for s8 in range(ng):
                            rot = (s8 - gl) % ng
                            r = a[s8] if rot == 0 else pltpu.roll(a[s8], rot * SSM_GROUP, axis=1)
                            t = r if t is None else jnp.where(masks[s8], r, t)
                        tiles.append(t)
                    halves.append(tiles)
                for gl in range(ng):
                    blk = jnp.concatenate([halves[0][gl], halves[1][gl]], axis=0).astype(BF16)
                    u_ref[j * ng + gl, clp * 2 * SUBLANES:(clp + 1) * 2 * SUBLANES,
                          hv * LANES:(hv + 1) * LANES] = blk


def _in_proj_kernel(x_ref, pos_ref, invf_ref, lng_ref, lnb_ref, win_ref, qg_ref, wq_ref,
                    kvg_ref, wkv_ref, xs_ref, u_ref, gs_ref, gm_ref, q_ref, k_ref, v_ref,
                    xs_scr, *, q_scale):
    nb, tl, d = x_ref.shape
    tm = nb * tl
    sw = xs_ref.shape[-1]
    xn = _layer_norm(x_ref[...].reshape(tm, d), lng_ref[...], lnb_ref[...]).astype(BF16)
    xs = jnp.dot(xn, win_ref[:, 0:sw], preferred_element_type=F32)
    for j in range(sw // LANES):
        xs_scr[j] = xs[:, j * LANES:(j + 1) * LANES]
    xs_ref[...] = xs.astype(BF16).reshape(nb, tl, sw)
    gs_ref[...] = jnp.dot(xn, win_ref[:, sw:2 * sw],
                          preferred_element_type=F32).astype(BF16).reshape(nb, tl, sw)
    gm_ref[...] = jnp.dot(xn, win_ref[:, 2 * sw:3 * sw],
                          preferred_element_type=F32).astype(BF16).reshape(nb, tl, sw)
    lat = jnp.dot(xn, win_ref[:, 3 * sw:], preferred_element_type=F32)
    cq = lat[:, 0:Q_LORA]
    ckv = lat[:, Q_LORA:Q_LORA + KV_LORA]
    krt = lat[:, Q_LORA + KV_LORA:]

    ang = pos_ref[...].reshape(tm, LANES).astype(F32) * invf_ref[...]
    lane = lax.broadcasted_iota(jnp.int32, ang.shape, 1)
    sn = jnp.sin(ang)
    c2 = jnp.where(lane < QK_ROPE, jnp.cos(ang), jnp.where(lane < QK_ROPE + QK_ROPE // 2, -sn, sn))

    cqn = _rms_norm(cq, qg_ref[...] * q_scale).astype(BF16)
    qf = jnp.dot(cqn, wq_ref[...], preferred_element_type=F32)
    for h in range(MLA_HEADS):
        o = h * HEAD_PAD
        q_ref[:, :, o:o + QK_NOPE] = qf[:, o:o + QK_NOPE].astype(BF16).reshape(nb, tl, QK_NOPE)
        q_ref[:, :, o + QK_NOPE:o + HEAD_PAD] = _rotate_pair_tile(
            qf[:, o + QK_NOPE:o + HEAD_PAD], c2, lane).astype(BF16).reshape(nb, tl, LANES)

    ckvn = _rms_norm(ckv, kvg_ref[...]).astype(BF16)
    kvf = jnp.dot(ckvn, wkv_ref[...], preferred_element_type=F32)
    k_rope = _rotate_pair_tile(krt, c2, lane).astype(BF16).reshape(nb, tl, LANES)
    nv = MLA_HEADS * QK_NOPE
    v_ref[...] = kvf[:, nv:].astype(BF16).reshape(nb, tl, MLA_HEADS * V_HEAD)
    for h in range(MLA_HEADS):
        o = h * HEAD_PAD
        k_ref[:, :, o:o + QK_NOPE] = kvf[:, h * QK_NOPE:(h + 1) * QK_NOPE].astype(BF16).reshape(
            nb, tl, QK_NOPE)
        k_ref[:, :, o + QK_NOPE:o + HEAD_PAD] = k_rope

    _rows_to_chunk_lanes(xs_scr, u_ref, tl)


def _in_proj_call(x, pos_b, invf, lng, lnb, win, qg, wq, kvg, wkv, *, tl):
    bsz, seq, d = x.shape
    sw = (win.shape[1] - (Q_LORA + KV_LORA + 2 * QK_ROPE)) // 3
    groups = sw // SSM_GROUP
    row = lambda w: pl.BlockSpec((bsz, tl, w), lambda i: (0, i, 0))
    full = lambda a: pl.BlockSpec(a.shape, lambda i: (0,) * a.ndim)
    qk_w = MLA_HEADS * HEAD_PAD
    u_rows = tl // S5_CHUNK * bsz
    act = lambda w: jax.ShapeDtypeStruct((bsz, seq, w), BF16)
    u_shape = jax.ShapeDtypeStruct((groups, seq // S5_CHUNK * bsz, S5_CHUNK * SSM_GROUP), BF16)
    return pl.pallas_call(
        functools.partial(_in_proj_kernel,
                          q_scale=(QK_NOPE + QK_ROPE) ** -0.5 * math.log2(math.e)),
        grid=(seq // tl,),
        in_specs=[row(d), row(LANES), full(invf), full(lng), full(lnb), full(win), full(qg),
                  full(wq), full(kvg), full(wkv)],
        out_specs=(row(sw), pl.BlockSpec((groups, u_rows, S5_CHUNK * SSM_GROUP), lambda i: (0, i, 0)),
                   row(sw), row(sw), row(qk_w), row(qk_w), row(MLA_HEADS * V_HEAD)),
        out_shape=(act(sw), u_shape, act(sw), act(sw), act(qk_w), act(qk_w),
                   act(MLA_HEADS * V_HEAD)),
        scratch_shapes=[pltpu.VMEM((sw // LANES, bsz * tl, LANES), F32)],
        compiler_params=pltpu.CompilerParams(dimension_semantics=("arbitrary",),
                                             vmem_limit_bytes=VMEM_LIMIT),
        name="in_proj",
    )(x, pos_b, invf, lng, lnb, win, qg, wq, kvg, wkv)


def _s5_kernel(u_ref, t_ref, wst_ref, wot_ref, av_ref, y_ref, s_ref, ssw_ref, hp_ref,
               *, gb, n_chunks, bsz):
    for g in range(gb):
        s = jnp.dot(u_ref[g], wst_ref[g], preferred_element_type=F32)
        s_ref[g] = s
        ssw_ref[g] = pltpu.roll(s, SSM_STATE, axis=1)

    ar = [av_ref[g, 0:1, :] for g in range(gb)]
    ai = [av_ref[g, 1:2, :] for g in range(gb)]
    ais = [av_ref[g, 2:3, :] for g in range(gb)]

    def step(c, carry):
        r = pl.multiple_of(c * bsz, bsz)
        new = []
        for g in range(gb):
            h, hsw = carry[2 * g], carry[2 * g + 1]
            hp_ref[g, pl.ds(r, bsz), :] = h
            new.append(ar[g] * h + ai[g] * hsw + s_ref[g, pl.ds(r, bsz), :])
            new.append(ar[g] * hsw + ais[g] * h + ssw_ref[g, pl.ds(r, bsz), :])
        return tuple(new)

    zero = jnp.zeros((bsz, 2 * SSM_STATE), F32)
    lax.fori_loop(0, n_chunks, step, (zero,) * (2 * gb))

    for g in range(gb):
        y = jnp.dot(u_ref[g], t_ref[g], preferred_element_type=F32)
        y = y + lax.dot_general(hp_ref[g].astype(BF16), wot_ref[g], (((1,), (1,)), ((), ())),
                                preferred_element_type=F32)
        y_ref[g] = y.astype(BF16)


def _s5_call(u, tmat, wst, wot, avec, *, bsz, gb):
    groups, rows, kw = u.shape
    n_chunks = rows // bsz
    blk = lambda a: pl.BlockSpec((gb,) + a.shape[1:], lambda i: (i,) + (0,) * (a.ndim - 1))
    return pl.pallas_call(
        functools.partial(_s5_kernel, gb=gb, n_chunks=n_chunks, bsz=bsz),
        grid=(groups // gb,),
        in_specs=[blk(u), blk(tmat), blk(wst), blk(wot), blk(avec)],
        out_specs=blk(u),
        out_shape=jax.ShapeDtypeStruct(u.shape, BF16),
        scratch_shapes=[pltpu.VMEM((gb, rows, 2 * SSM_STATE), F32),
                        pltpu.VMEM((gb, rows, 2 * SSM_STATE), F32),
                        pltpu.VMEM((gb, rows, 2 * SSM_STATE), F32)],
        compiler_params=pltpu.CompilerParams(dimension_semantics=("arbitrary",),
                                             vmem_limit_bytes=VMEM_LIMIT),
        name="s5",
    )(u, tmat, wst, wot, avec)


def _s5_prep_kernel(par_ref, bt1_ref, bt2_ref, cx_ref, cy_ref, t_ref, wst_ref, wot_ref, av_ref,
                    *, gb):
    tc, cg = S5_CHUNK, SSM_GROUP
    lane = lax.broadcasted_iota(jnp.int32, (1, LANES), 1)
    sgn = jnp.where(lane < SSM_STATE, -1.0, 1.0)
    tau = lax.broadcasted_iota(jnp.int32, (tc + SUBLANES, LANES), 0).astype(F32)
    lane_t = lax.broadcasted_iota(jnp.int32, (cg, LANES), 1)
    for g in range(gb):
        a_r, a_i = par_ref[g, 0:1, :], par_ref[g, 1:2, :]
        dt = jnp.exp(par_ref[g, 2:3, :])
        mag = jnp.exp(tau * (dt * a_r))
        ang = tau * (dt * a_i)
        pr = mag * jnp.cos(ang)
        pi = mag * jnp.sin(ang)
        pis = sgn * pi
        ab_r, ab_i = pr[1:2], pi[1:2]
        den = a_r * a_r + a_i * a_i
        nr = ab_r - 1.0
        k_r = (nr * a_r + ab_i * a_i) / den
        k_is = sgn * ((ab_i * a_r - nr * a_i) / den)
        bt1, bt2 = bt1_ref[g], bt2_ref[g]
        bk1 = k_r * bt1 + k_is * bt2
        bk2 = k_r * bt2 - k_is * bt1
        cx, cy = cx_ref[g], cy_ref[g]
        for s in range(tc):
            wst_ref[g, s * cg:(s + 1) * cg, :] = (
                pr[tc - 1 - s:tc - s] * bk1 + pis[tc - 1 - s:tc - s] * bk2).astype(BF16)
        for t in range(tc):
            wot_ref[g, t * cg:(t + 1) * cg, :] = (
                cx * pr[t + 1:t + 2] + cy * pi[t + 1:t + 2]).astype(BF16)
        rgt = jnp.concatenate([cx * pr[t:t + 1] + cy * pi[t:t + 1] for t in range(tc)], axis=0)
        m_all = lax.dot_general(bk1, rgt, (((1,), (1,)), ((), ())),
                                precision=lax.Precision.HIGHEST, preferred_element_type=F32)
        lo, hi = m_all[:, :LANES], m_all[:, LANES:]
        per_tile = LANES // cg
        for s in range(tc):
            k = (s % per_tile) * cg
            if k == 0:
                r_lo, r_hi = lo, hi
            else:
                rl, rh = pltpu.roll(lo, k, axis=1), pltpu.roll(hi, k, axis=1)
                r_lo, r_hi = jnp.where(lane_t >= k, rl, 0.0), jnp.where(lane_t >= k, rh, rl)
            if s >= per_tile:
                r_lo, r_hi = jnp.zeros_like(lo), r_lo
            t_ref[g, s * cg:(s + 1) * cg, :] = jnp.concatenate([r_lo, r_hi], axis=1).astype(BF16)
        av_ref[g] = jnp.concatenate(
            [pr[tc:tc + 1], pis[tc:tc + 1], -pis[tc:tc + 1],
             jnp.zeros((SUBLANES - 3, LANES), F32)], axis=0)


def _s5_prep_call(a_re, a_im, log_dt, b_re, b_im, c_re, c_im, *, gb):
    g = a_re.shape[0]
    dup = lambda v: jnp.concatenate([v, v], axis=-1)
    par = jnp.stack([dup(a_re), dup(a_im), jnp.broadcast_to(log_dt[:, None], (g, LANES))]
                    + [jnp.zeros((g, LANES), F32)] * (SUBLANES - 3), axis=1)
    b_re_t, b_im_t = b_re.transpose(0, 2, 1), b_im.transpose(0, 2, 1)
    bt1 = jnp.concatenate([b_re_t, b_im_t], axis=-1)
    bt2 = jnp.concatenate([b_im_t, b_re_t], axis=-1)
    cx = jnp.concatenate([c_re, -c_im], axis=-1)
    cy = jnp.concatenate([-c_im, -c_re], axis=-1)
    kw = S5_CHUNK * SSM_GROUP
    blk = lambda a: pl.BlockSpec((gb,) + a.shape[1:], lambda i: (i,) + (0,) * (a.ndim - 1))
    oblk = lambda r, c: pl.BlockSpec((gb, r, c), lambda i: (i, 0, 0))
    return pl.pallas_call(
        functools.partial(_s5_prep_kernel, gb=gb),
        grid=(g // gb,),
        in_specs=[blk(par), blk(bt1), blk(bt2), blk(cx), blk(cy)],
        out_specs=(oblk(kw, kw), oblk(kw, 2 * SSM_STATE), oblk(kw, 2 * SSM_STATE),
                   oblk(SUBLANES, LANES)),
        out_shape=(jax.ShapeDtypeStruct((g, kw, kw), BF16),
                   jax.ShapeDtypeStruct((g, kw, 2 * SSM_STATE), BF16),
                   jax.ShapeDtypeStruct((g, kw, 2 * SSM_STATE), BF16),
                   jax.ShapeDtypeStruct((g, SUBLANES, LANES), F32)),
        compiler_params=pltpu.CompilerParams(dimension_semantics=("arbitrary",)),
        name="s5_prep",
    )(par, bt1, bt2, cx, cy)


def _attn_kernel(q_ref, k_ref, v_ref, o_ref, vt_scr, st_scr, p_scr, acc_scr, *, tq):
    i = pl.program_id(1)
    n_kb = k_ref.shape[0] // tq

    @pl.when(i == 0)
    def _():
        ones = jnp.ones((VT_ROWS - V_HEAD, tq), BF16)
        for h in range(MLA_HEADS):
            for jb in range(n_kb):
                blk = v_ref[jb * tq:(jb + 1) * tq, h * V_HEAD:(h + 1) * V_HEAD]
                vt_scr[h, jb, 0:V_HEAD, :] = blk.astype(F32).T.astype(BF16)
                vt_scr[h, jb, V_HEAD:VT_ROWS, :] = ones

    heads = range(MLA_HEADS)
    cols = [slice(h * HEAD_PAD, (h + 1) * HEAD_PAD) for h in heads]

    def scores(j, slot, h):
        r = pl.multiple_of(j * tq, tq)
        st_scr[slot, h] = lax.dot_general(k_ref[pl.ds(r, tq), cols[h]], q_ref[:, cols[h]],
                                          (((1,), (1,)), ((), ())), preferred_element_type=F32)

    def softmax(slot, h, m, masked):
        s = st_scr[slot, h]
        if masked:
            ki = lax.broadcasted_iota(jnp.int32, s.shape, 0)
            qi = lax.broadcasted_iota(jnp.int32, s.shape, 1)
            s = jnp.where(ki <= qi, s, -jnp.inf)
        m_new = jnp.maximum(m, jnp.max(s, axis=0, keepdims=True))
        p_scr[slot, h] = jnp.exp2(s - m_new).astype(BF16)
        return jnp.exp2(m - m_new), m_new

    def accumulate(j, slot, h, alpha):
        acc_scr[h] = alpha * acc_scr[h] + jnp.dot(vt_scr[h, j], p_scr[slot, h],
                                                  preferred_element_type=F32)

    def trip(j, slot, carry):
        alpha, m = carry
        new_alpha, new_m = [], []
        for h in heads:
            scores(j + 1, 1 - slot, h)
            accumulate(jnp.maximum(j - 1, 0), 1 - slot, h, alpha[h])
            a, mm = softmax(slot, h, m[h], False)
            new_alpha.append(a)
            new_m.append(mm)
        return new_alpha, new_m

    def finish(slot, carry):
        alpha, m = carry
        for h in heads:
            accumulate(jnp.maximum(i - 1, 0), 1 - slot, h, alpha[h])
            a, _ = softmax(slot, h, m[h], True)
            accumulate(i, slot, h, a)
            acc = acc_scr[h]
            o_t = acc[:V_HEAD] / acc[V_HEAD:V_HEAD + 1]
            o_ref[:, h * V_HEAD:(h + 1) * V_HEAD] = o_t.T.astype(BF16)

    p_scr[1] = jnp.zeros(p_scr.shape[1:], BF16)
    acc_scr[...] = jnp.zeros(acc_scr.shape, F32)
    for h in heads:
        scores(0, 0, h)
    init = ([jnp.ones((1, tq), F32) for _ in heads], [jnp.full((1, tq), -jnp.inf, F32) for _ in heads])
    carry = lax.fori_loop(0, lax.shift_right_logical(i, 1),
                          lambda t, c: trip(2 * t + 1, 1, trip(2 * t, 0, c)), init)

    @pl.when((i & 1) == 0)
    def _():
        finish(0, carry)

    @pl.when((i & 1) == 1)
    def _():
        finish(1, trip(i - 1, 0, carry))


def _attn_call(q, k, v, *, tq):
    bsz, seq, w = q.shape
    return pl.pallas_call(
        functools.partial(_attn_kernel, tq=tq),
        grid=(bsz, seq // tq),
        in_specs=[pl.BlockSpec((None, tq, w), lambda b, i: (b, i, 0)),
                  pl.BlockSpec((None, seq, w), lambda b, i: (b, 0, 0)),
                  pl.BlockSpec((None, seq, v.shape[-1]), lambda b, i: (b, 0, 0))],
        out_specs=pl.BlockSpec((None, tq, MLA_HEADS * V_HEAD), lambda b, i: (b, i, 0)),
        out_shape=jax.ShapeDtypeStruct((bsz, seq, MLA_HEADS * V_HEAD), BF16),
        scratch_shapes=[pltpu.VMEM((MLA_HEADS, seq // tq, VT_ROWS, tq), BF16),
                        pltpu.VMEM((2, MLA_HEADS, tq, tq), F32),
                        pltpu.VMEM((2, MLA_HEADS, tq, tq), BF16),
                        pltpu.VMEM((MLA_HEADS, VT_ROWS, tq), F32)],
        compiler_params=pltpu.CompilerParams(dimension_semantics=("arbitrary", "arbitrary"),
                                             vmem_limit_bytes=VMEM_LIMIT),
        name="attn",
    )(q, k, v)


def _gelu_tanh(x):
    return 0.5 * x * (1.0 + jnp.tanh(math.sqrt(2.0 / math.pi) * (x + 0.044715 * (x * x * x))))


def _chunk_lanes_to_rows(y_ref, y_scr, nb, tl):
    masks = _lane_group_masks()
    ng = GROUPS_PER_TILE
    n_tiles = y_ref.shape[0] // ng
    for j in range(n_tiles):
        yg = [y_ref[j * ng + gl].astype(F32) for gl in range(ng)]
        for cl in range(tl // S5_CHUNK):
            for hv in range(S5_CHUNK // ng):
                src = [yg[gl][cl * SUBLANES:(cl + 1) * SUBLANES, hv * LANES:(hv + 1) * LANES]
                       for gl in range(ng)]
                for s8 in range(ng):
                    t = None
                    for gl in range(ng):
                        rot = (gl - s8) % ng
                        r = src[gl] if rot == 0 else pltpu.roll(src[gl], rot * SSM_GROUP, axis=1)
                        t = r if t is None else jnp.where(masks[gl], r, t)
                    step = cl * S5_CHUNK + hv * ng + s8
                    y_scr[j, step * nb:(step + 1) * nb, :] = t
    return jnp.concatenate(
        [jnp.concatenate([y_scr[j, pl.ds(b, tl, stride=nb), :] for b in range(nb)], axis=0)
         for j in range(n_tiles)], axis=1)


def _out_kernel(x_ref, p_ref, yssm_ref, xs_ref, gs_ref, o_ref, gm_ref, lng0_ref, lnb0_ref,
                dskip_ref, wglu_ref, bglu_ref, wout_ref, wpg_ref, wpp_ref, lng_ref, lnb_ref,
                out_ref, y_scr):
    nb, tl, d = x_ref.shape
    tm = nb * tl
    sw = xs_ref.shape[-1]
    flat = lambda ref: ref[...].reshape(tm, ref.shape[-1])
    xn = _layer_norm(flat(x_ref), lng0_ref[...], lnb0_ref[...])
    yssm = _chunk_lanes_to_rows(yssm_ref, y_scr, nb, tl)
    y = yssm + dskip_ref[...] * flat(xs_ref).astype(F32)
    y = _gelu_tanh(y)
    glu = jnp.dot(y.astype(BF16), wglu_ref[...], preferred_element_type=F32) + bglu_ref[...]
    gs = flat(gs_ref).astype(F32)
    ys = y * jax.nn.sigmoid(glu) * (gs * jax.nn.sigmoid(gs))
    gm = flat(gm_ref).astype(F32)
    ym = flat(o_ref).astype(F32) * (gm * jax.nn.sigmoid(gm))
    mix = (jnp.dot(ys.astype(BF16), wout_ref[0:sw, :], preferred_element_type=F32)
           + jnp.dot(ym.astype(BF16), wout_ref[sw:, :], preferred_element_type=F32))
    u = DEEPNORM_ALPHA * xn + mix
    gate = jax.nn.sigmoid(jnp.dot(u.astype(BF16), wpg_ref[...], preferred_element_type=F32))
    ple = gate * jnp.dot(flat(p_ref).astype(BF16), wpp_ref[...], preferred_element_type=F32)
    out_ref[...] = _layer_norm(u + ple, lng_ref[...], lnb_ref[...]).reshape(nb, tl, d)


def _out_call(x, p3, yg, xs, gs, o, gm, lng0, lnb0, dskip, wglu, bglu, wout, wpg, wpp,
              lng, lnb, *, tl):
    bsz, seq, d = x.shape
    row = lambda a: pl.BlockSpec((bsz, tl, a.shape[-1]), lambda i: (0, i, 0))
    full = lambda a: pl.BlockSpec(a.shape, lambda i: (0,) * a.ndim)
    y_spec = pl.BlockSpec((yg.shape[0], tl // S5_CHUNK * bsz, yg.shape[2]), lambda i: (0, i, 0))
    params = (lng0, lnb0, dskip, wglu, bglu, wout, wpg, wpp, lng, lnb)
    return pl.pallas_call(
        _out_kernel,
        grid=(seq // tl,),
        in_specs=[row(x), row(p3), y_spec, row(xs), row(gs), row(o), row(gm)]
                 + [full(a) for a in params],
        out_specs=pl.BlockSpec((bsz, tl, d), lambda i: (0, i, 0)),
        out_shape=jax.ShapeDtypeStruct((bsz, seq, d), F32),
        scratch_shapes=[pltpu.VMEM((xs.shape[-1] // LANES, bsz * tl, LANES), F32)],
        compiler_params=pltpu.CompilerParams(dimension_semantics=("arbitrary",),
                                             vmem_limit_bytes=VMEM_LIMIT),
        name="out_proj",
    )(x, p3, yg, xs, gs, o, gm, *params)


def _swap_halves_cols(w):
    half = w.shape[-1] // 2
    return jnp.concatenate([w[..., half:], w[..., :half]], axis=-1)


def kernel(x, p, positions, ln_emb_g, ln_emb_b, w_in, a_re, a_im, log_dt, b_re, b_im, c_re,
           c_im, d_skip, w_glu, b_glu, q_norm_g, w_uq, kv_norm_g, w_ukv, w_out, w_pg, w_pp,
           ln_g, ln_b):
    bsz, seq, d = x.shape
    assert bsz == SUBLANES, "the S5 recurrence keeps the batch on the 8 sublanes of a vreg"
    sw = d_skip.shape[-1]
    mw = MLA_HEADS * V_HEAD
    row2 = lambda a: a.reshape(1, -1)

    wi = w_in[0]
    o_cq = 2 * sw
    o_ckv = o_cq + Q_LORA
    o_kr = o_ckv + KV_LORA
    o_gm = o_kr + QK_ROPE
    kr_cols = wi[:, o_kr:o_gm]
    win = jnp.concatenate([wi[:, 0:sw], wi[:, sw:2 * sw], wi[:, o_gm:o_gm + mw],
                           wi[:, o_cq:o_ckv], wi[:, o_ckv:o_kr], kr_cols,
                           _swap_halves_cols(kr_cols)], axis=1).astype(BF16)
    wq3 = w_uq[0].reshape(Q_LORA, MLA_HEADS, QK_NOPE + QK_ROPE)
    wq = jnp.concatenate([wq3, _swap_halves_cols(wq3[..., QK_NOPE:])], axis=-1)
    wq = wq.reshape(Q_LORA, MLA_HEADS * HEAD_PAD).astype(BF16)
    wkv3 = w_ukv[0].reshape(KV_LORA, MLA_HEADS, QK_NOPE + V_HEAD)
    wkv = jnp.concatenate([wkv3[..., :QK_NOPE].reshape(KV_LORA, -1),
                           wkv3[..., QK_NOPE:].reshape(KV_LORA, -1)], axis=1).astype(BF16)

    inv_freq = 1.0 / (ROPE_THETA ** (np.arange(0, QK_ROPE, 2, dtype=np.float32) / QK_ROPE))
    invf = jnp.asarray(np.tile(inv_freq, LANES // inv_freq.size).reshape(1, LANES), F32)
    pos_b = jnp.broadcast_to(positions[:, :, None], (bsz, seq, LANES))

    xs, u, gs, gm, q, k, v = _in_proj_call(
        x, pos_b, invf, row2(ln_emb_g), row2(ln_emb_b), win, row2(q_norm_g[0]), wq,
        row2(kv_norm_g[0]), wkv, tl=ROW_TILE_STEPS)

    tmat, wst, wot, avec = _s5_prep_call(a_re[0], a_im[0], log_dt[0], b_re[0], b_im[0],
                                         c_re[0], c_im[0], gb=S5_GROUP_BLOCK)
    yg = _s5_call(u, tmat, wst, wot, avec, bsz=bsz, gb=S5_GROUP_BLOCK)

    o = _attn_call(q, k, v, tq=ATTN_TILE)

    return _out_call(
        x, p[0], yg, xs, gs, o, gm, row2(ln_emb_g), row2(ln_emb_b),
        row2(d_skip[0]), w_glu[0].astype(BF16), row2(b_glu[0]), w_out[0].astype(BF16),
        w_pg[0].astype(BF16), w_pp[0].astype(BF16), row2(ln_g[0]), row2(ln_b[0]),
        tl=ROW_TILE_STEPS)
```

```python
import functools
import math

import numpy as np
import jax
import jax.numpy as jnp
from jax import lax
from jax.experimental import pallas as pl
from jax.experimental.pallas import tpu as pltpu

F32 = jnp.float32
BF16 = jnp.bfloat16

SSM_GROUP = 16
SSM_STATE = 64
MLA_HEADS = 4
QK_NOPE = 128
QK_ROPE = 64
V_HEAD = 128
Q_LORA = 256
KV_LORA = 128
ROPE_THETA = 10000.0
LN_EPS = 1e-5
RMS_EPS = 1e-6
DEPTH = 1
DEEPNORM_ALPHA = (2 * DEPTH) ** 0.25

LANES = 128
SUBLANES = 8
HEAD_PAD = 2 * LANES
VT_ROWS = V_HEAD + 16
S5_CHUNK = 16
GROUPS_PER_TILE = LANES // SSM_GROUP
VMEM_LIMIT = 56 * 1024 * 1024
ROW_TILE_STEPS = 64
ATTN_TILE = 256
S5_GROUP_BLOCK = 4
ROW_PITCH_PAD = 4


def _layer_norm(x, g, b):
    mu = jnp.mean(x, axis=-1, keepdims=True)
    xc = x - mu
    var = jnp.mean(xc * xc, axis=-1, keepdims=True)
    return xc * lax.rsqrt(var + LN_EPS) * g + b


def _rms_norm(x, g):
    return x * lax.rsqrt(jnp.mean(x * x, axis=-1, keepdims=True) + RMS_EPS) * g


def _rotate_pair_tile(t, c2, lane):
    p = t * c2
    return jnp.where(lane < QK_ROPE, p + pltpu.roll(p, QK_ROPE, axis=1), 0.0)


def _lane_group_masks():
    grp = lax.broadcasted_iota(jnp.int32, (2 * SUBLANES, LANES), 1) // SSM_GROUP
    return [grp == k for k in range(GROUPS_PER_TILE)]


def _rows_to_chunk_lanes(xs_scr, u_ref, tl):
    masks = _lane_group_masks()
    ng = GROUPS_PER_TILE
    pitch = xs_scr.shape[1] // SUBLANES
    for clp in range(tl // (2 * S5_CHUNK)):
        for hv in range(S5_CHUNK // ng):
            for j in range(xs_scr.shape[0]):
                a = []
                for s8 in range(ng):
                    rows = [xs_scr[j, pl.ds(cl * S5_CHUNK + hv * ng + s8, SUBLANES, stride=pitch), :]
                            for cl in (2 * clp, 2 * clp + 1)]
                    a.append(jnp.concatenate(rows, axis=0).astype(BF16))
                for gl in range(ng):
                    t = None
                    for s8 in range(ng):
                        rot = (s8 - gl) % ng
                        r = a[s8] if rot == 0 else pltpu.roll(a[s8], rot * SSM_GROUP, axis=1)
                        t = r if t is None else jnp.where(masks[s8], r, t)
                    u_ref[j * ng + gl, clp * 2 * SUBLANES:(clp + 1) * 2 * SUBLANES,
                          hv * LANES:(hv + 1) * LANES] = t


def _in_proj_kernel(x_ref, pos_ref, invf_ref, lng_ref, lnb_ref, win_ref, qg_ref, wq_ref,
                    kvg_ref, wkv_ref, xn_ref, xs_ref, u_ref, gs_ref, gm_ref, q_ref, k_ref, v_ref,
                    xs_scr, *, q_scale):
    nb, tl, d = x_ref.shape
    tm = nb * tl
    sw = xs_ref.shape[-1]
    xn = _layer_norm(x_ref[...].reshape(tm, d), lng_ref[...], lnb_ref[...])
    xn_ref[...] = xn.reshape(nb, tl, d)
    xn = xn.astype(BF16)

    lat = jnp.dot(xn, win_ref[:, 3 * sw:], preferred_element_type=F32)
    cq = lat[:, 0:Q_LORA]
    ckv = lat[:, Q_LORA:Q_LORA + KV_LORA]
    krt = lat[:, Q_LORA + KV_LORA:]

    ang = pos_ref[...].reshape(tm // 2, LANES).astype(F32) * invf_ref[0:1, :] + invf_ref[1:2, :]
    t0 = jnp.sin(ang)
    quarter = LANES // 4
    t1, t2, t3 = (pltpu.roll(t0, k * quarter, axis=1) for k in (1, 2, 3))
    lane = lax.broadcasted_iota(jnp.int32, t0.shape, 1)
    pick = lambda q0, q1, q2, q3: jnp.where(
        lane < quarter, q0, jnp.where(lane < 2 * quarter, q1, jnp.where(lane < 3 * quarter, q2, q3)))
    c2 = jnp.concatenate([pick(t0, t1, -t1, t2), pick(t2, t3, -t3, t0)], axis=0)
    lane = lax.broadcasted_iota(jnp.int32, c2.shape, 1)

    cqn = _rms_norm(cq, qg_ref[...] * q_scale).astype(BF16)
    qf = jnp.dot(cqn, wq_ref[...], preferred_element_type=F32)
    for h in range(MLA_HEADS):
        o = h * HEAD_PAD
        q_ref[:, :, o:o + QK_NOPE] = qf[:, o:o + QK_NOPE].astype(BF16).reshape(nb, tl, QK_NOPE)
        q_ref[:, :, o + QK_NOPE:o + HEAD_PAD] = _rotate_pair_tile(
            qf[:, o + QK_NOPE:o + HEAD_PAD], c2, lane).astype(BF16).reshape(nb, tl, LANES)

    ckvn = _rms_norm(ckv, kvg_ref[...]).astype(BF16)
    kvf = jnp.dot(ckvn, wkv_ref[...], preferred_element_type=F32)
    k_rope = _rotate_pair_tile(krt, c2, lane).astype(BF16).reshape(nb, tl, LANES)
    nv = MLA_HEADS * QK_NOPE
    v_ref[...] = kvf[:, nv:].astype(BF16).reshape(nb, tl, MLA_HEADS * V_HEAD)
    for h in range(MLA_HEADS):
        o = h * HEAD_PAD
        k_ref[:, :, o:o + QK_NOPE] = kvf[:, h * QK_NOPE:(h + 1) * QK_NOPE].astype(BF16).reshape(
            nb, tl, QK_NOPE)
        k_ref[:, :, o + QK_NOPE:o + HEAD_PAD] = k_rope

    xs = jnp.dot(xn, win_ref[:, 0:sw], preferred_element_type=F32)
    pitch = xs_scr.shape[1] // nb
    for j in range(sw // LANES):
        for b in range(nb):
            xs_scr[j, b * pitch:b * pitch + tl, :] = xs[b * tl:(b + 1) * tl, j * LANES:(j + 1) * LANES]
    xs_ref[...] = xs.astype(BF16).reshape(nb, tl, sw)
    gs_ref[...] = jnp.dot(xn, win_ref[:, sw:2 * sw],
                          preferred_element_type=F32).astype(BF16).reshape(nb, tl, sw)
    _rows_to_chunk_lanes(xs_scr, u_ref, tl)
    gm_ref[...] = jnp.dot(xn, win_ref[:, 2 * sw:3 * sw],
                          preferred_element_type=F32).astype(BF16).reshape(nb, tl, sw)


def _in_proj_call(x, pos_b, invf, lng, lnb, win, qg, wq, kvg, wkv, *, tl):
    bsz, seq, d = x.shape
    sw = (win.shape[1] - (Q_LORA + KV_LORA + 2 * QK_ROPE)) // 3
    groups = sw // SSM_GROUP
    row = lambda w: pl.BlockSpec((bsz, tl, w), lambda i: (0, i, 0))
    full = lambda a: pl.BlockSpec(a.shape, lambda i: (0,) * a.ndim)
    qk_w = MLA_HEADS * HEAD_PAD
    u_rows = tl // S5_CHUNK * bsz
    act = lambda w: jax.ShapeDtypeStruct((bsz, seq, w), BF16)
    u_shape = jax.ShapeDtypeStruct((groups, seq // S5_CHUNK * bsz, S5_CHUNK * SSM_GROUP), BF16)
    return pl.pallas_call(
        functools.partial(_in_proj_kernel,
                          q_scale=(QK_NOPE + QK_ROPE) ** -0.5 * math.log2(math.e)),
        grid=(seq // tl,),
        in_specs=[row(d), pl.BlockSpec((bsz // 2, tl, LANES), lambda i: (0, i, 0)),
                  full(invf), full(lng), full(lnb), full(win), full(qg),
                  full(wq), full(kvg), full(wkv)],
        out_specs=(row(d), row(sw),
                   pl.BlockSpec((groups, u_rows, S5_CHUNK * SSM_GROUP), lambda i: (0, i, 0)),
                   row(sw), row(sw), row(qk_w), row(qk_w), row(MLA_HEADS * V_HEAD)),
        out_shape=(jax.ShapeDtypeStruct((bsz, seq, d), F32), act(sw), u_shape, act(sw), act(sw),
                   act(qk_w), act(qk_w), act(MLA_HEADS * V_HEAD)),
        scratch_shapes=[pltpu.VMEM((sw // LANES, bsz * (tl + ROW_PITCH_PAD), LANES), F32)],
        compiler_params=pltpu.CompilerParams(dimension_semantics=("arbitrary",),
                                             vmem_limit_bytes=VMEM_LIMIT),
        name="in_proj",
    )(x, pos_b, invf, lng, lnb, win, qg, wq, kvg, wkv)


def _s5_kernel(u_ref, t_ref, wst_ref, wot_ref, av_ref, y_ref, s_ref, ssw_ref, hp_ref,
               *, gb, n_chunks, bsz):
    for g in range(gb):
        s = jnp.dot(u_ref[g], wst_ref[g], preferred_element_type=F32)
        s_ref[g] = s
        ssw_ref[g] = pltpu.roll(s, SSM_STATE, axis=1)

    ar = [av_ref[g, 0:1, :] for g in range(gb)]
    ai = [av_ref[g, 1:2, :] for g in range(gb)]
    ais = [av_ref[g, 2:3, :] for g in range(gb)]

    def step(c, carry):
        r = pl.multiple_of(c * bsz, bsz)
        new = []
        for g in range(gb):
            h, hsw = carry[2 * g], carry[2 * g + 1]
            hp_ref[g, pl.ds(r, bsz), :] = h
            new.append(ar[g] * h + ai[g] * hsw + s_ref[g, pl.ds(r, bsz), :])
            new.append(ar[g] * hsw + ais[g] * h + ssw_ref[g, pl.ds(r, bsz), :])
        return tuple(new)

    zero = jnp.zeros((bsz, 2 * SSM_STATE), F32)
    lax.fori_loop(0, n_chunks, step, (zero,) * (2 * gb))

    for g in range(gb):
        y = jnp.dot(u_ref[g], t_ref[g], preferred_element_type=F32)
        y = y + lax.dot_general(hp_ref[g].astype(BF16), wot_ref[g], (((1,), (1,)), ((), ())),
                                preferred_element_type=F32)
        y_ref[g] = y.astype(BF16)


def _s5_call(u, tmat, wst, wot, avec, *, bsz, gb):
    groups, rows, kw = u.shape
    n_chunks = rows // bsz
    blk = lambda a: pl.BlockSpec((gb,) + a.shape[1:], lambda i: (i,) + (0,) * (a.ndim - 1))
    return pl.pallas_call(
        functools.partial(_s5_kernel, gb=gb, n_chunks=n_chunks, bsz=bsz),
        grid=(groups // gb,),
        in_specs=[blk(u), blk(tmat), blk(wst), blk(wot), blk(avec)],
        out_specs=blk(u),
        out_shape=jax.ShapeDtypeStruct(u.shape, BF16),
        scratch_shapes=[pltpu.VMEM((gb, rows, 2 * SSM_STATE), F32),
                        pltpu.VMEM((gb, rows, 2 * SSM_STATE), F32),
                        pltpu.VMEM((gb, rows, 2 * SSM_STATE), F32)],
        compiler_params=pltpu.CompilerParams(dimension_semantics=("arbitrary",),
                                             vmem_limit_bytes=VMEM_LIMIT),
        name="s5",
    )(u, tmat, wst, wot, avec)


def _s5_prep_kernel(par_ref, bt1_ref, bt2_ref, cx_ref, cy_ref, t_ref, wst_ref, wot_ref, av_ref,
                    *, gb):
    tc, cg = S5_CHUNK, SSM_GROUP
    lane = lax.broadcasted_iota(jnp.int32, (1, LANES), 1)
    sgn = jnp.where(lane < SSM_STATE, -1.0, 1.0)
    tau = lax.broadcasted_iota(jnp.int32, (tc + SUBLANES, LANES), 0).astype(F32)
    lane_t = lax.broadcasted_iota(jnp.int32, (cg, LANES), 1)
    for g in range(gb):
        a_r, a_i = par_ref[g, 0:1, :], par_ref[g, 1:2, :]
        dt = jnp.exp(par_ref[g, 2:3, :])
        mag = jnp.exp(tau * (dt * a_r))
        ang = tau * (dt * a_i)
        pr = mag * jnp.cos(ang)
        pi = mag * jnp.sin(ang)
        pis = sgn * pi
        ab_r, ab_i = pr[1:2], pi[1:2]
        den = a_r * a_r + a_i * a_i
        nr = ab_r - 1.0
        k_r = (nr * a_r + ab_i * a_i) / den
        k_is = sgn * ((ab_i * a_r - nr * a_i) / den)
        bt1, bt2 = bt1_ref[g], bt2_ref[g]
        bk1 = k_r * bt1 + k_is * bt2
        bk2 = k_r * bt2 - k_is * bt1
        cx, cy = cx_ref[g], cy_ref[g]
        for s in range(tc):
            wst_ref[g, s * cg:(s + 1) * cg, :] = (
                pr[tc - 1 - s:tc - s] * bk1 + pis[tc - 1 - s:tc - s] * bk2).astype(BF16)
        for t in range(tc):
            wot_ref[g, t * cg:(t + 1) * cg, :] = (
                cx * pr[t + 1:t + 2] + cy * pi[t + 1:t + 2]).astype(BF16)
        rgt = jnp.concatenate([cx * pr[t:t + 1] + cy * pi[t:t + 1] for t in range(tc)], axis=0)
        m_all = lax.dot_general(bk1, rgt, (((1,), (1,)), ((), ())),
                                precision=lax.Precision.HIGHEST, preferred_element_type=F32)
        lo, hi = m_all[:, :LANES], m_all[:, LANES:]
        per_tile = LANES // cg
        for s in range(tc):
            k = (s % per_tile) * cg
            if k == 0:
                r_lo, r_hi = lo, hi
            else:
                rl, rh = pltpu.roll(lo, k, axis=1), pltpu.roll(hi, k, axis=1)
                r_lo, r_hi = jnp.where(lane_t >= k, rl, 0.0), jnp.where(lane_t >= k, rh, rl)
            if s >= per_tile:
                r_lo, r_hi = jnp.zeros_like(lo), r_lo
            t_ref[g, s * cg:(s + 1) * cg, :] = jnp.concatenate([r_lo, r_hi], axis=1).astype(BF16)
        av_ref[g] = jnp.concatenate(
            [pr[tc:tc + 1], pis[tc:tc + 1], -pis[tc:tc + 1],
             jnp.zeros((SUBLANES - 3, LANES), F32)], axis=0)


def _s5_prep_call(a_re, a_im, log_dt, b_re, b_im, c_re, c_im, *, gb):
    g = a_re.shape[0]
    dup = lambda v: jnp.concatenate([v, v], axis=-1)
    par = jnp.stack([dup(a_re), dup(a_im), jnp.broadcast_to(log_dt[:, None], (g, LANES))]
                    + [jnp.zeros((g, LANES), F32)] * (SUBLANES - 3), axis=1)
    b_re_t, b_im_t = b_re.transpose(0, 2, 1), b_im.transpose(0, 2, 1)
    bt1 = jnp.concatenate([b_re_t, b_im_t], axis=-1)
    bt2 = jnp.concatenate([b_im_t, b_re_t], axis=-1)
    cx = jnp.concatenate([c_re, -c_im], axis=-1)
    cy = jnp.concatenate([-c_im, -c_re], axis=-1)
    kw = S5_CHUNK * SSM_GROUP
    blk = lambda a: pl.BlockSpec((gb,) + a.shape[1:], lambda i: (i,) + (0,) * (a.ndim - 1))
    oblk = lambda r, c: pl.BlockSpec((gb, r, c), lambda i: (i, 0, 0))
    return pl.pallas_call(
        functools.partial(_s5_prep_kernel, gb=gb),
        grid=(g // gb,),
        in_specs=[blk(par), blk(bt1), blk(bt2), blk(cx), blk(cy)],
        out_specs=(oblk(kw, kw), oblk(kw, 2 * SSM_STATE), oblk(kw, 2 * SSM_STATE),
                   oblk(SUBLANES, LANES)),
        out_shape=(jax.ShapeDtypeStruct((g, kw, kw), BF16),
                   jax.ShapeDtypeStruct((g, kw, 2 * SSM_STATE), BF16),
                   jax.ShapeDtypeStruct((g, kw, 2 * SSM_STATE), BF16),
                   jax.ShapeDtypeStruct((g, SUBLANES, LANES), F32)),
        compiler_params=pltpu.CompilerParams(dimension_semantics=("arbitrary",)),
        name="s5_prep",
    )(par, bt1, bt2, cx, cy)


def _attn_kernel(q_ref, k_ref, v_ref, o_ref, vt_scr, st_scr, p_scr, acc_scr, *, tq):
    i = pl.program_id(1)
    n_kb = k_ref.shape[0] // tq

    @pl.when(i == 0)
    def _():
        ones = jnp.ones((VT_ROWS - V_HEAD, tq), BF16)
        for h in range(MLA_HEADS):
            for jb in range(n_kb):
                blk = v_ref[jb * tq:(jb + 1) * tq, h * V_HEAD:(h + 1) * V_HEAD]
                vt_scr[h, jb, 0:V_HEAD, :] = blk.astype(F32).T.astype(BF16)
                vt_scr[h, jb, V_HEAD:VT_ROWS, :] = ones

    heads = range(MLA_HEADS)
    cols = [slice(h * HEAD_PAD, (h + 1) * HEAD_PAD) for h in heads]

    def scores(j, slot, h):
        r = pl.multiple_of(j * tq, tq)
        st_scr[slot, h] = lax.dot_general(k_ref[pl.ds(r, tq), cols[h]], q_ref[:, cols[h]],
                                          (((1,), (1,)), ((), ())), preferred_element_type=F32)

    def softmax(slot, h, m, masked):
        s = st_scr[slot, h]
        if masked:
            ki = lax.broadcasted_iota(jnp.int32, s.shape, 0)
            qi = lax.broadcasted_iota(jnp.int32, s.shape, 1)
            s = jnp.where(ki <= qi, s, -jnp.inf)
        m_new = jnp.maximum(m, jnp.max(s, axis=0, keepdims=True))
        p_scr[slot, h] = jnp.exp2(s - m_new).astype(BF16)
        return jnp.exp2(m - m_new), m_new

    def accumulate(j, slot, h, alpha):
        acc_scr[h] = alpha * acc_scr[h] + jnp.dot(vt_scr[h, j], p_scr[slot, h],
                                                  preferred_element_type=F32)

    def trip(j, slot, carry):
        alpha, m = carry
        new_alpha, new_m = [], []
        for h in heads:
            scores(j + 1, 1 - slot, h)
            accumulate(jnp.maximum(j - 1, 0), 1 - slot, h, alpha[h])
            a, mm = softmax(slot, h, m[h], False)
            new_alpha.append(a)
            new_m.append(mm)
        return new_alpha, new_m

    def finish(slot, carry):
        alpha, m = carry
        for h in heads:
            accumulate(jnp.maximum(i - 1, 0), 1 - slot, h, alpha[h])
            a, _ = softmax(slot, h, m[h], True)
            accumulate(i, slot, h, a)
            acc = acc_scr[h]
            o_t = acc[:V_HEAD] / acc[V_HEAD:V_HEAD + 1]
            o_ref[:, h * V_HEAD:(h + 1) * V_HEAD] = o_t.T.astype(BF16)

    p_scr[1] = jnp.zeros(p_scr.shape[1:], BF16)
    acc_scr[...] = jnp.zeros(acc_scr.shape, F32)
    for h in heads:
        scores(0, 0, h)
    init = ([jnp.ones((1, tq), F32) for _ in heads], [jnp.full((1, tq), -jnp.inf, F32) for _ in heads])
    carry = lax.fori_loop(0, lax.shift_right_logical(i, 1),
                          lambda t, c: trip(2 * t + 1, 1, trip(2 * t, 0, c)), init)

    @pl.when((i & 1) == 0)
    def _():
        finish(0, carry)

    @pl.when((i & 1) == 1)
    def _():
        finish(1, trip(i - 1, 0, carry))


def _attn_call(q, k, v, *, tq):
    bsz, seq, w = q.shape
    return pl.pallas_call(
        functools.partial(_attn_kernel, tq=tq),
        grid=(bsz, seq // tq),
        in_specs=[pl.BlockSpec((None, tq, w), lambda b, i: (b, i, 0)),
                  pl.BlockSpec((None, seq, w), lambda b, i: (b, 0, 0)),
                  pl.BlockSpec((None, seq, v.shape[-1]), lambda b, i: (b, 0, 0))],
        out_specs=pl.BlockSpec((None, tq, MLA_HEADS * V_HEAD), lambda b, i: (b, i, 0)),
        out_shape=jax.ShapeDtypeStruct((bsz, seq, MLA_HEADS * V_HEAD), BF16),
        scratch_shapes=[pltpu.VMEM((MLA_HEADS, seq // tq, VT_ROWS, tq), BF16),
                        pltpu.VMEM((2, MLA_HEADS, tq, tq), F32),
                        pltpu.VMEM((2, MLA_HEADS, tq, tq), BF16),
                        pltpu.VMEM((MLA_HEADS, VT_ROWS, tq), F32)],
        compiler_params=pltpu.CompilerParams(dimension_semantics=("arbitrary", "arbitrary"),
                                             vmem_limit_bytes=VMEM_LIMIT),
        name="attn",
    )(q, k, v)


def _gelu_tanh(x):
    return 0.5 * x * (1.0 + jnp.tanh(math.sqrt(2.0 / math.pi) * (x + 0.044715 * (x * x * x))))


def _chunk_lanes_to_rows(y_ref, y_scr, nb, tl):
    masks = _lane_group_masks()
    ng = GROUPS_PER_TILE
    n_tiles = y_ref.shape[0] // ng
    for j in range(n_tiles):
        for clp in range(tl // (2 * S5_CHUNK)):
            for hv in range(S5_CHUNK // ng):
                src = [y_ref[j * ng + gl, clp * 2 * SUBLANES:(clp + 1) * 2 * SUBLANES,
                             hv * LANES:(hv + 1) * LANES] for gl in range(ng)]
                for s8 in range(ng):
                    t = None
                    for gl in range(ng):
                        rot = (gl - s8) % ng
                        r = src[gl] if rot == 0 else pltpu.roll(src[gl], rot * SSM_GROUP, axis=1)
                        t = r if t is None else jnp.where(masks[gl], r, t)
                    t = t.astype(F32)
                    for half, cl in enumerate((2 * clp, 2 * clp + 1)):
                        step = cl * S5_CHUNK + hv * ng + s8
                        y_scr[j, step * nb:(step + 1) * nb, :] = t[half * SUBLANES:
                                                                   (half + 1) * SUBLANES]
    return jnp.concatenate(
        [jnp.concatenate([y_scr[j, pl.ds(b, tl, stride=nb), :] for b in range(nb)], axis=0)
         for j in range(n_tiles)], axis=1)


def _sigmoid(x):
    return 0.5 * jnp.tanh(0.5 * x) + 0.5


def _silu(x):
    h = 0.5 * x
    return h * jnp.tanh(h) + h


def _out_kernel(xn_ref, p_ref, yssm_ref, xs_ref, gs_ref, o_ref, gm_ref,
                dskip_ref, wglu_ref, bglu_ref, wout_ref, wpg_ref, wpp_ref, lng_ref, lnb_ref,
                out_ref, y_scr):
    nb, tl, d = xn_ref.shape
    tm = nb * tl
    sw = xs_ref.shape[-1]
    flat = lambda ref: ref[...].reshape(tm, ref.shape[-1])
    ym = flat(o_ref) * _silu(flat(gm_ref))
    mix_m = jnp.dot(ym, wout_ref[sw:, :], preferred_element_type=F32)
    pp = jnp.dot(flat(p_ref).astype(BF16), wpp_ref[...], preferred_element_type=F32)
    yssm = _chunk_lanes_to_rows(yssm_ref, y_scr, nb, tl)
    y = yssm + dskip_ref[...] * flat(xs_ref).astype(F32)
    y = _gelu_tanh(y)
    glu = jnp.dot(y.astype(BF16), wglu_ref[...], preferred_element_type=F32) + bglu_ref[...]
    ys = (y * _sigmoid(glu)).astype(BF16) * _silu(flat(gs_ref))
    mix = jnp.dot(ys, wout_ref[0:sw, :], preferred_element_type=F32) + mix_m
    u = DEEPNORM_ALPHA * flat(xn_ref) + mix
    hb = nb // 2
    for r in range(2):
        rows = slice(r * hb * tl, (r + 1) * hb * tl)
        ur = u[rows]
        gate = _sigmoid(jnp.dot(ur.astype(BF16), wpg_ref[...], preferred_element_type=F32))
        out_ref[r * hb:(r + 1) * hb] = _layer_norm(ur + gate * pp[rows], lng_ref[...],
                                                   lnb_ref[...]).reshape(hb, tl, d)


def _out_call(xn, p3, yg, xs, gs, o, gm, dskip, wglu, bglu, wout, wpg, wpp, lng, lnb, *, tl):
    bsz, seq, d = xn.shape
    row = lambda a: pl.BlockSpec((bsz, tl, a.shape[-1]), lambda i: (0, i, 0))
    full = lambda a: pl.BlockSpec(a.shape, lambda i: (0,) * a.ndim)
    y_spec = pl.BlockSpec((yg.shape[0], tl // S5_CHUNK * bsz, yg.shape[2]), lambda i: (0, i, 0))
    params = (dskip, wglu, bglu, wout, wpg, wpp, lng, lnb)
    return pl.pallas_call(
        _out_kernel,
        grid=(seq // tl,),
        in_specs=[row(xn), row(p3), y_spec, row(xs), row(gs), row(o), row(gm)]
                 + [full(a) for a in params],
        out_specs=pl.BlockSpec((bsz, tl, d), lambda i: (0, i, 0)),
        out_shape=jax.ShapeDtypeStruct((bsz, seq, d), F32),
        scratch_shapes=[pltpu.VMEM((xs.shape[-1] // LANES, bsz * tl, LANES), F32)],
        compiler_params=pltpu.CompilerParams(dimension_semantics=("arbitrary",),
                                             vmem_limit_bytes=VMEM_LIMIT),
        name="out_proj",
    )(xn, p3, yg, xs, gs, o, gm, *params)


def _swap_halves_cols(w):
    half = w.shape[-1] // 2
    return jnp.concatenate([w[..., half:], w[..., :half]], axis=-1)


def kernel(x, p, positions, ln_emb_g, ln_emb_b, w_in, a_re, a_im, log_dt, b_re, b_im, c_re,
           c_im, d_skip, w_glu, b_glu, q_norm_g, w_uq, kv_norm_g, w_ukv, w_out, w_pg, w_pp,
           ln_g, ln_b):
    bsz, seq, d = x.shape
    assert bsz == SUBLANES, "the S5 recurrence keeps the batch on the 8 sublanes of a vreg"
    sw = d_skip.shape[-1]
    mw = MLA_HEADS * V_HEAD
    row2 = lambda a: a.reshape(1, -1)

    wi = w_in[0]
    o_cq = 2 * sw
    o_ckv = o_cq + Q_LORA
    o_kr = o_ckv + KV_LORA
    o_gm = o_kr + QK_ROPE
    kr_cols = wi[:, o_kr:o_gm]
    win = jnp.concatenate([wi[:, 0:sw], wi[:, sw:2 * sw], wi[:, o_gm:o_gm + mw],
                           wi[:, o_cq:o_ckv], wi[:, o_ckv:o_kr], kr_cols,
                           _swap_halves_cols(kr_cols)], axis=1).astype(BF16)
    wq3 = w_uq[0].reshape(Q_LORA, MLA_HEADS, QK_NOPE + QK_ROPE)
    wq = jnp.concatenate([wq3, _swap_halves_cols(wq3[..., QK_NOPE:])], axis=-1)
    wq = wq.reshape(Q_LORA, MLA_HEADS * HEAD_PAD).astype(BF16)
    wkv3 = w_ukv[0].reshape(KV_LORA, MLA_HEADS, QK_NOPE + V_HEAD)
    wkv = jnp.concatenate([wkv3[..., :QK_NOPE].reshape(KV_LORA, -1),
                           wkv3[..., QK_NOPE:].reshape(KV_LORA, -1)], axis=1).astype(BF16)

    inv_freq = 1.0 / (ROPE_THETA ** (np.arange(0, QK_ROPE, 2, dtype=np.float32) / QK_ROPE))
    n_rep = LANES // inv_freq.size
    rope_phase = np.repeat(np.array([np.pi / 2, 0.0, np.pi / 2, 0.0], np.float32), inv_freq.size)
    invf = jnp.asarray(np.stack([np.tile(inv_freq, n_rep), rope_phase]), F32)
    half = LANES // 2
    pos_b = jnp.concatenate(
        [jnp.broadcast_to(positions[:bsz // 2, :, None], (bsz // 2, seq, half)),
         jnp.broadcast_to(positions[bsz // 2:, :, None], (bsz // 2, seq, half))], axis=-1)

    xn, xs, u, gs, gm, q, k, v = _in_proj_call(
        x, pos_b, invf, row2(ln_emb_g), row2(ln_emb_b), win, row2(q_norm_g[0]), wq,
        row2(kv_norm_g[0]), wkv, tl=ROW_TILE_STEPS)

    tmat, wst, wot, avec = _s5_prep_call(a_re[0], a_im[0], log_dt[0], b_re[0], b_im[0],
                                         c_re[0], c_im[0], gb=S5_GROUP_BLOCK)
    yg = _s5_call(u, tmat, wst, wot, avec, bsz=bsz, gb=S5_GROUP_BLOCK)

    o = _attn_call(q, k, v, tq=ATTN_TILE)

    return _out_call(
        xn, p[0], yg, xs, gs, o, gm,
        row2(d_skip[0]), w_glu[0].astype(BF16), row2(b_glu[0]), w_out[0].astype(BF16),
        w_pg[0].astype(BF16), w_pp[0].astype(BF16), row2(ln_g[0]), row2(ln_b[0]),
        tl=ROW_TILE_STEPS)
```

```python
import functools
import math

import numpy as np
import jax
import jax.numpy as jnp
from jax import lax
from jax.experimental import pallas as pl
from jax.experimental.pallas import tpu as pltpu

F32 = jnp.float32
BF16 = jnp.bfloat16

SSM_GROUP = 16
SSM_STATE = 64
MLA_HEADS = 4
QK_NOPE = 128
QK_ROPE = 64
V_HEAD = 128
Q_LORA = 256
KV_LORA = 128
ROPE_THETA = 10000.0
LN_EPS = 1e-5
RMS_EPS = 1e-6
DEPTH = 1
DEEPNORM_ALPHA = (2 * DEPTH) ** 0.25

LANES = 128
SUBLANES = 8
HEAD_PAD = 2 * LANES
VT_ROWS = V_HEAD + 16
S5_CHUNK = 16
GROUPS_PER_TILE = LANES // SSM_GROUP
VMEM_LIMIT = 56 * 1024 * 1024
ROW_TILE_STEPS = 64
ATTN_Q_TILE = 512
ATTN_K_TILE = 256
S5_GROUP_BLOCK = 4
ROW_PITCH_PAD = 4


def _layer_norm(x, g, b):
    mu = jnp.mean(x, axis=-1, keepdims=True)
    xc = x - mu
    var = jnp.mean(xc * xc, axis=-1, keepdims=True)
    return xc * lax.rsqrt(var + LN_EPS) * g + b


def _rms_norm(x, g):
    return x * lax.rsqrt(jnp.mean(x * x, axis=-1, keepdims=True) + RMS_EPS) * g


def _rotate_pair_tile(t, c2, lane):
    p = t * c2
    return jnp.where(lane < QK_ROPE, p + pltpu.roll(p, QK_ROPE, axis=1), 0.0)


def _lane_group_masks():
    grp = lax.broadcasted_iota(jnp.int32, (2 * SUBLANES, LANES), 1) // SSM_GROUP
    return [grp == k for k in range(GROUPS_PER_TILE)]


def _rows_to_chunk_lanes(xs_scr, u_ref, tl):
    masks = _lane_group_masks()
    ng = GROUPS_PER_TILE
    pitch = xs_scr.shape[1] // SUBLANES
    for clp in range(tl // (2 * S5_CHUNK)):
        for hv in range(S5_CHUNK // ng):
            for j in range(xs_scr.shape[0]):
                a = []
                for s8 in range(ng):
                    rows = [xs_scr[j, pl.ds(cl * S5_CHUNK + hv * ng + s8, SUBLANES, stride=pitch), :]
                            for cl in (2 * clp, 2 * clp + 1)]
                    a.append(jnp.concatenate(rows, axis=0).astype(BF16))
                for gl in range(ng):
                    t = None
                    for s8 in range(ng):
                        rot = (s8 - gl) % ng
                        r = a[s8] if rot == 0 else pltpu.roll(a[s8], rot * SSM_GROUP, axis=1)
                        t = r if t is None else jnp.where(masks[s8], r, t)
                    u_ref[j * ng + gl, clp * 2 * SUBLANES:(clp + 1) * 2 * SUBLANES,
                          hv * LANES:(hv + 1) * LANES] = t


def _in_proj_kernel(x_ref, pos_ref, invf_ref, lng_ref, lnb_ref, win_ref, qg_ref, wq_ref,
                    kvg_ref, wkv_ref, xn_ref, xs_ref, u_ref, gs_ref, gm_ref, q_ref, k_ref, v_ref,
                    xs_scr, *, q_scale):
    nb, tl, d = x_ref.shape
    tm = nb * tl
    sw = xs_ref.shape[-1]
    xn = _layer_norm(x_ref[...].reshape(tm, d), lng_ref[...], lnb_ref[...])
    xn_ref[...] = xn.reshape(nb, tl, d)
    xn = xn.astype(BF16)

    lat = jnp.dot(xn, win_ref[:, 3 * sw:], preferred_element_type=F32)
    cq = lat[:, 0:Q_LORA]
    ckv = lat[:, Q_LORA:Q_LORA + KV_LORA]
    krt = lat[:, Q_LORA + KV_LORA:]

    ang = pos_ref[...].reshape(tm // 2, LANES).astype(F32) * invf_ref[0:1, :] + invf_ref[1:2, :]
    t0 = jnp.sin(ang)
    quarter = LANES // 4
    t1, t2, t3 = (pltpu.roll(t0, k * quarter, axis=1) for k in (1, 2, 3))
    lane = lax.broadcasted_iota(jnp.int32, t0.shape, 1)
    pick = lambda q0, q1, q2, q3: jnp.where(
        lane < quarter, q0, jnp.where(lane < 2 * quarter, q1, jnp.where(lane < 3 * quarter, q2, q3)))
    c2 = jnp.concatenate([pick(t0, t1, -t1, t2), pick(t2, t3, -t3, t0)], axis=0)
    lane = lax.broadcasted_iota(jnp.int32, c2.shape, 1)

    cqn = _rms_norm(cq, qg_ref[...] * q_scale).astype(BF16)
    qf = jnp.dot(cqn, wq_ref[...], preferred_element_type=F32)
    for h in range(MLA_HEADS):
        o = h * HEAD_PAD
        q_ref[:, :, o:o + QK_NOPE] = qf[:, o:o + QK_NOPE].astype(BF16).reshape(nb, tl, QK_NOPE)
        q_ref[:, :, o + QK_NOPE:o + HEAD_PAD] = _rotate_pair_tile(
            qf[:, o + QK_NOPE:o + HEAD_PAD], c2, lane).astype(BF16).reshape(nb, tl, LANES)

    ckvn = _rms_norm(ckv, kvg_ref[...]).astype(BF16)
    kvf = jnp.dot(ckvn, wkv_ref[...], preferred_element_type=F32)
    k_rope = _rotate_pair_tile(krt, c2, lane).astype(BF16).reshape(nb, tl, LANES)
    nv = MLA_HEADS * QK_NOPE
    v_ref[...] = kvf[:, nv:].astype(BF16).reshape(nb, tl, MLA_HEADS * V_HEAD)
    for h in range(MLA_HEADS):
        o = h * HEAD_PAD
        k_ref[:, :, o:o + QK_NOPE] = kvf[:, h * QK_NOPE:(h + 1) * QK_NOPE].astype(BF16).reshape(
            nb, tl, QK_NOPE)
        k_ref[:, :, o + QK_NOPE:o + HEAD_PAD] = k_rope

    xs = jnp.dot(xn, win_ref[:, 0:sw], preferred_element_type=F32)
    pitch = xs_scr.shape[1] // nb
    for j in range(sw // LANES):
        for b in range(nb):
            xs_scr[j, b * pitch:b * pitch + tl, :] = xs[b * tl:(b + 1) * tl, j * LANES:(j + 1) * LANES]
    xs_ref[...] = xs.astype(BF16).reshape(nb, tl, sw)
    gs_ref[...] = jnp.dot(xn, win_ref[:, sw:2 * sw],
                          preferred_element_type=F32).astype(BF16).reshape(nb, tl, sw)
    _rows_to_chunk_lanes(xs_scr, u_ref, tl)
    gm_ref[...] = jnp.dot(xn, win_ref[:, 2 * sw:3 * sw],
                          preferred_element_type=F32).astype(BF16).reshape(nb, tl, sw)


def _in_proj_call(x, pos_b, invf, lng, lnb, win, qg, wq, kvg, wkv, *, tl):
    bsz, seq, d = x.shape
    sw = (win.shape[1] - (Q_LORA + KV_LORA + 2 * QK_ROPE)) // 3
    groups = sw // SSM_GROUP
    row = lambda w: pl.BlockSpec((bsz, tl, w), lambda i: (0, i, 0))
    full = lambda a: pl.BlockSpec(a.shape, lambda i: (0,) * a.ndim)
    qk_w = MLA_HEADS * HEAD_PAD
    u_rows = tl // S5_CHUNK * bsz
    act = lambda w: jax.ShapeDtypeStruct((bsz, seq, w), BF16)
    u_shape = jax.ShapeDtypeStruct((groups, seq // S5_CHUNK * bsz, S5_CHUNK * SSM_GROUP), BF16)
    return pl.pallas_call(
        functools.partial(_in_proj_kernel,
                          q_scale=(QK_NOPE + QK_ROPE) ** -0.5 * math.log2(math.e)),
        grid=(seq // tl,),
        in_specs=[row(d), pl.BlockSpec((bsz // 2, tl, LANES), lambda i: (0, i, 0)),
                  full(invf), full(lng), full(lnb), full(win), full(qg),
                  full(wq), full(kvg), full(wkv)],
        out_specs=(row(d), row(sw),
                   pl.BlockSpec((groups, u_rows, S5_CHUNK * SSM_GROUP), lambda i: (0, i, 0)),
                   row(sw), row(sw), row(qk_w), row(qk_w), row(MLA_HEADS * V_HEAD)),
        out_shape=(jax.ShapeDtypeStruct((bsz, seq, d), F32), act(sw), u_shape, act(sw), act(sw),
                   act(qk_w), act(qk_w), act(MLA_HEADS * V_HEAD)),
        scratch_shapes=[pltpu.VMEM((sw // LANES, bsz * (tl + ROW_PITCH_PAD), LANES), F32)],
        compiler_params=pltpu.CompilerParams(dimension_semantics=("arbitrary",),
                                             vmem_limit_bytes=VMEM_LIMIT),
        name="in_proj",
    )(x, pos_b, invf, lng, lnb, win, qg, wq, kvg, wkv)


def _s5_kernel(u_ref, t_ref, wst_ref, wot_ref, av_ref, y_ref, s_ref, ssw_ref, hp_ref,
               *, gb, n_chunks, bsz):
    for g in range(gb):
        s = jnp.dot(u_ref[g], wst_ref[g], preferred_element_type=F32)
        s_ref[g] = s
        ssw_ref[g] = pltpu.roll(s, SSM_STATE, axis=1)

    ar = [av_ref[g, 0:1, :] for g in range(gb)]
    ai = [av_ref[g, 1:2, :] for g in range(gb)]
    ais = [av_ref[g, 2:3, :] for g in range(gb)]

    def step(c, carry):
        r = pl.multiple_of(c * bsz, bsz)
        new = []
        for g in range(gb):
            h, hsw = carry[2 * g], carry[2 * g + 1]
            hp_ref[g, pl.ds(r, bsz), :] = h
            new.append(ar[g] * h + ai[g] * hsw + s_ref[g, pl.ds(r, bsz), :])
            new.append(ar[g] * hsw + ais[g] * h + ssw_ref[g, pl.ds(r, bsz), :])
        return tuple(new)

    zero = jnp.zeros((bsz, 2 * SSM_STATE), F32)
    lax.fori_loop(0, n_chunks, step, (zero,) * (2 * gb))

    for g in range(gb):
        y = jnp.dot(u_ref[g], t_ref[g], preferred_element_type=F32)
        y = y + lax.dot_general(hp_ref[g].astype(BF16), wot_ref[g], (((1,), (1,)), ((), ())),
                                preferred_element_type=F32)
        y_ref[g] = y.astype(BF16)


def _s5_call(u, tmat, wst, wot, avec, *, bsz, gb):
    groups, rows, kw = u.shape
    n_chunks = rows // bsz
    blk = lambda a: pl.BlockSpec((gb,) + a.shape[1:], lambda i: (i,) + (0,) * (a.ndim - 1))
    return pl.pallas_call(
        functools.partial(_s5_kernel, gb=gb, n_chunks=n_chunks, bsz=bsz),
        grid=(groups // gb,),
        in_specs=[blk(u), blk(tmat), blk(wst), blk(wot), blk(avec)],
        out_specs=blk(u),
        out_shape=jax.ShapeDtypeStruct(u.shape, BF16),
        scratch_shapes=[pltpu.VMEM((gb, rows, 2 * SSM_STATE), F32),
                        pltpu.VMEM((gb, rows, 2 * SSM_STATE), F32),
                        pltpu.VMEM((gb, rows, 2 * SSM_STATE), F32)],
        compiler_params=pltpu.CompilerParams(dimension_semantics=("arbitrary",),
                                             vmem_limit_bytes=VMEM_LIMIT),
        name="s5",
    )(u, tmat, wst, wot, avec)


def _s5_prep_kernel(par_ref, bt1_ref, bt2_ref, cx_ref, cy_ref, t_ref, wst_ref, wot_ref, av_ref,
                    *, gb):
    tc, cg = S5_CHUNK, SSM_GROUP
    lane = lax.broadcasted_iota(jnp.int32, (1, LANES), 1)
    sgn = jnp.where(lane < SSM_STATE, -1.0, 1.0)
    tau = lax.broadcasted_iota(jnp.int32, (tc + SUBLANES, LANES), 0).astype(F32)
    lane_t = lax.broadcasted_iota(jnp.int32, (cg, LANES), 1)
    for g in range(gb):
        a_r, a_i = par_ref[g, 0:1, :], par_ref[g, 1:2, :]
        dt = jnp.exp(par_ref[g, 2:3, :])
        mag = jnp.exp(tau * (dt * a_r))
        ang = tau * (dt * a_i)
        pr = mag * jnp.cos(ang)
        pi = mag * jnp.sin(ang)
        pis = sgn * pi
        ab_r, ab_i = pr[1:2], pi[1:2]
        den = a_r * a_r + a_i * a_i
        nr = ab_r - 1.0
        k_r = (nr * a_r + ab_i * a_i) / den
        k_is = sgn * ((ab_i * a_r - nr * a_i) / den)
        bt1, bt2 = bt1_ref[g], bt2_ref[g]
        bk1 = k_r * bt1 + k_is * bt2
        bk2 = k_r * bt2 - k_is * bt1
        cx, cy = cx_ref[g], cy_ref[g]
        for s in range(tc):
            wst_ref[g, s * cg:(s + 1) * cg, :] = (
                pr[tc - 1 - s:tc - s] * bk1 + pis[tc - 1 - s:tc - s] * bk2).astype(BF16)
        for t in range(tc):
            wot_ref[g, t * cg:(t + 1) * cg, :] = (
                cx * pr[t + 1:t + 2] + cy * pi[t + 1:t + 2]).astype(BF16)
        rgt = jnp.concatenate([cx * pr[t:t + 1] + cy * pi[t:t + 1] for t in range(tc)], axis=0)
        m_all = lax.dot_general(bk1, rgt, (((1,), (1,)), ((), ())),
                                precision=lax.Precision.HIGHEST, preferred_element_type=F32)
        lo, hi = m_all[:, :LANES], m_all[:, LANES:]
        per_tile = LANES // cg
        for s in range(tc):
            k = (s % per_tile) * cg
            if k == 0:
                r_lo, r_hi = lo, hi
            else:
                rl, rh = pltpu.roll(lo, k, axis=1), pltpu.roll(hi, k, axis=1)
                r_lo, r_hi = jnp.where(lane_t >= k, rl, 0.0), jnp.where(lane_t >= k, rh, rl)
            if s >= per_tile:
                r_lo, r_hi = jnp.zeros_like(lo), r_lo
            t_ref[g, s * cg:(s + 1) * cg, :] = jnp.concatenate([r_lo, r_hi], axis=1).astype(BF16)
        av_ref[g] = jnp.concatenate(
            [pr[tc:tc + 1], pis[tc:tc + 1], -pis[tc:tc + 1],
             jnp.zeros((SUBLANES - 3, LANES), F32)], axis=0)


def _s5_prep_call(a_re, a_im, log_dt, b_re, b_im, c_re, c_im, *, gb):
    g = a_re.shape[0]
    dup = lambda v: jnp.concatenate([v, v], axis=-1)
    par = jnp.stack([dup(a_re), dup(a_im), jnp.broadcast_to(log_dt[:, None], (g, LANES))]
                    + [jnp.zeros((g, LANES), F32)] * (SUBLANES - 3), axis=1)
    b_re_t, b_im_t = b_re.transpose(0, 2, 1), b_im.transpose(0, 2, 1)
    bt1 = jnp.concatenate([b_re_t, b_im_t], axis=-1)
    bt2 = jnp.concatenate([b_im_t, b_re_t], axis=-1)
    cx = jnp.concatenate([c_re, -c_im], axis=-1)
    cy = jnp.concatenate([-c_im, -c_re], axis=-1)
    kw = S5_CHUNK * SSM_GROUP
    blk = lambda a: pl.BlockSpec((gb,) + a.shape[1:], lambda i: (i,) + (0,) * (a.ndim - 1))
    oblk = lambda r, c: pl.BlockSpec((gb, r, c), lambda i: (i, 0, 0))
    return pl.pallas_call(
        functools.partial(_s5_prep_kernel, gb=gb),
        grid=(g // gb,),
        in_specs=[blk(par), blk(bt1), blk(bt2), blk(cx), blk(cy)],
        out_specs=(oblk(kw, kw), oblk(kw, 2 * SSM_STATE), oblk(kw, 2 * SSM_STATE),
                   oblk(SUBLANES, LANES)),
        out_shape=(jax.ShapeDtypeStruct((g, kw, kw), BF16),
                   jax.ShapeDtypeStruct((g, kw, 2 * SSM_STATE), BF16),
                   jax.ShapeDtypeStruct((g, kw, 2 * SSM_STATE), BF16),
                   jax.ShapeDtypeStruct((g, SUBLANES, LANES), F32)),
        compiler_params=pltpu.CompilerParams(dimension_semantics=("arbitrary",)),
        name="s5_prep",
    )(par, bt1, bt2, cx, cy)


def _attn_kernel(q_ref, k_ref, v_ref, o_ref, vt_scr, st_scr, p_scr, acc_scr, alpha_scr, m_scr,
                 *, tq, tk):
    i = pl.program_id(1)
    n_kb = k_ref.shape[0] // tk

    @pl.when(i == 0)
    def _():
        ones = jnp.ones((VT_ROWS - V_HEAD, tk), BF16)
        for h in range(MLA_HEADS):
            for jb in range(n_kb):
                blk = v_ref[jb * tk:(jb + 1) * tk, h * V_HEAD:(h + 1) * V_HEAD]
                vt_scr[h, jb, 0:V_HEAD, :] = blk.astype(F32).T.astype(BF16)
                vt_scr[h, jb, V_HEAD:VT_ROWS, :] = ones

    heads = range(MLA_HEADS)
    cols = [slice(h * HEAD_PAD, (h + 1) * HEAD_PAD) for h in heads]

    def scores(j, slot, h):
        r = pl.multiple_of(j * tk, tk)
        st_scr[slot, h] = lax.dot_general(k_ref[pl.ds(r, tk), cols[h]], q_ref[:, cols[h]],
                                          (((1,), (1,)), ((), ())), preferred_element_type=F32)

    def softmax(slot, h, m, diag_offset):
        s = st_scr[slot, h]
        if diag_offset is not None:
            ki = lax.broadcasted_iota(jnp.int32, s.shape, 0)
            qi = lax.broadcasted_iota(jnp.int32, s.shape, 1)
            s = jnp.where(ki + diag_offset <= qi, s, -jnp.inf)
        m_new = jnp.maximum(m, jnp.max(s, axis=0, keepdims=True))
        p_scr[slot, h] = jnp.exp2(s - m_new).astype(BF16)
        return jnp.exp2(m - m_new), m_new

    def accumulate(j, slot, h, alpha):
        acc_scr[h] = alpha * acc_scr[h] + jnp.dot(vt_scr[h, j], p_scr[slot, h],
                                                  preferred_element_type=F32)

    def trip(j, slot, diag_offset=None):
        for h in heads:
            scores(j + 1, 1 - slot, h)
            accumulate(jnp.maximum(j - 1, 0), 1 - slot, h, alpha_scr[h])
            alpha_scr[h], m_scr[h] = softmax(slot, h, m_scr[h], diag_offset)

    p_scr[1] = jnp.zeros(p_scr.shape[1:], BF16)
    acc_scr[...] = jnp.zeros(acc_scr.shape, F32)
    alpha_scr[...] = jnp.ones(alpha_scr.shape, F32)
    m_scr[...] = jnp.full(m_scr.shape, -jnp.inf, F32)
    for h in heads:
        scores(0, 0, h)

    @pl.loop(0, i)
    def _(t):
        trip(2 * t, 0)
        trip(2 * t + 1, 1)

    trip(2 * i, 0, diag_offset=0)
    for h in heads:
        accumulate(2 * i, 0, h, alpha_scr[h])
        a, _ = softmax(1, h, m_scr[h], tk)
        accumulate(2 * i + 1, 1, h, a)
        acc = acc_scr[h]
        o_t = acc[:V_HEAD] / acc[V_HEAD:V_HEAD + 1]
        o_ref[:, h * V_HEAD:(h + 1) * V_HEAD] = o_t.T.astype(BF16)


def _attn_call(q, k, v, *, tq, tk):
    bsz, seq, w = q.shape
    assert tq == 2 * tk
    return pl.pallas_call(
        functools.partial(_attn_kernel, tq=tq, tk=tk),
        grid=(bsz, seq // tq),
        in_specs=[pl.BlockSpec((None, tq, w), lambda b, i: (b, i, 0)),
                  pl.BlockSpec((None, seq, w), lambda b, i: (b, 0, 0)),
                  pl.BlockSpec((None, seq, v.shape[-1]), lambda b, i: (b, 0, 0))],
        out_specs=pl.BlockSpec((None, tq, MLA_HEADS * V_HEAD), lambda b, i: (b, i, 0)),
        out_shape=jax.ShapeDtypeStruct((bsz, seq, MLA_HEADS * V_HEAD), BF16),
        scratch_shapes=[pltpu.VMEM((MLA_HEADS, seq // tk, VT_ROWS, tk), BF16),
                        pltpu.VMEM((2, MLA_HEADS, tk, tq), F32),
                        pltpu.VMEM((2, MLA_HEADS, tk, tq), BF16),
                        pltpu.VMEM((MLA_HEADS, VT_ROWS, tq), F32),
                        pltpu.VMEM((MLA_HEADS, 1, tq), F32),
                        pltpu.VMEM((MLA_HEADS, 1, tq), F32)],
        compiler_params=pltpu.CompilerParams(dimension_semantics=("arbitrary", "arbitrary"),
                                             vmem_limit_bytes=VMEM_LIMIT),
        name="attn",
    )(q, k, v)


def _gelu_tanh(x):
    return 0.5 * x * (1.0 + jnp.tanh(math.sqrt(2.0 / math.pi) * (x + 0.044715 * (x * x * x))))


def _chunk_lanes_to_rows(y_ref, y_scr, nb, tl):
    masks = _lane_group_masks()
    ng = GROUPS_PER_TILE
    n_tiles = y_ref.shape[0] // ng
    for j in range(n_tiles):
        for clp in range(tl // (2 * S5_CHUNK)):
            for hv in range(S5_CHUNK // ng):
                src = [y_ref[j * ng + gl, clp * 2 * SUBLANES:(clp + 1) * 2 * SUBLANES,
                             hv * LANES:(hv + 1) * LANES] for gl in range(ng)]
                for s8 in range(ng):
                    t = None
                    for gl in range(ng):
                        rot = (gl - s8) % ng
                        r = src[gl] if rot == 0 else pltpu.roll(src[gl], rot * SSM_GROUP, axis=1)
                        t = r if t is None else jnp.where(masks[gl], r, t)
                    t = t.astype(F32)
                    for half, cl in enumerate((2 * clp, 2 * clp + 1)):
                        step = cl * S5_CHUNK + hv * ng + s8
                        y_scr[j, step * nb:(step + 1) * nb, :] = t[half * SUBLANES:
                                                                   (half + 1) * SUBLANES]
    return jnp.concatenate(
        [jnp.concatenate([y_scr[j, pl.ds(b, tl, stride=nb), :] for b in range(nb)], axis=0)
         for j in range(n_tiles)], axis=1)


def _sigmoid(x):
    return 0.5 * jnp.tanh(0.5 * x) + 0.5


def _silu(x):
    h = 0.5 * x
    return h * jnp.tanh(h) + h


def _out_kernel(xn_ref, p_ref, yssm_ref, xs_ref, gs_ref, o_ref, gm_ref,
                dskip_ref, wglu_ref, bglu_ref, wout_ref, wpg_ref, wpp_ref, lng_ref, lnb_ref,
                out_ref, y_scr):
    nb, tl, d = xn_ref.shape
    tm = nb * tl
    sw = xs_ref.shape[-1]
    flat = lambda ref: ref[...].reshape(tm, ref.shape[-1])
    ym = flat(o_ref) * _silu(flat(gm_ref))
    mix_m = jnp.dot(ym, wout_ref[sw:, :], preferred_element_type=F32)
    pp = jnp.dot(flat(p_ref).astype(BF16), wpp_ref[...], preferred_element_type=F32)
    yssm = _chunk_lanes_to_rows(yssm_ref, y_scr, nb, tl)
    y = yssm + dskip_ref[...] * flat(xs_ref).astype(F32)
    y = _gelu_tanh(y)
    glu = jnp.dot(y.astype(BF16), wglu_ref[...], preferred_element_type=F32) + bglu_ref[...]
    ys = (y * _sigmoid(glu)).astype(BF16) * _silu(flat(gs_ref))
    mix = jnp.dot(ys, wout_ref[0:sw, :], preferred_element_type=F32) + mix_m
    u = DEEPNORM_ALPHA * flat(xn_ref) + mix
    hb = nb // 2
    for r in range(2):
        rows = slice(r * hb * tl, (r + 1) * hb * tl)
        ur = u[rows]
        gate = _sigmoid(jnp.dot(ur.astype(BF16), wpg_ref[...], preferred_element_type=F32))
        out_ref[r * hb:(r + 1) * hb] = _layer_norm(ur + gate * pp[rows], lng_ref[...],
                                                   lnb_ref[...]).reshape(hb, tl, d)


def _out_call(xn, p3, yg, xs, gs, o, gm, dskip, wglu, bglu, wout, wpg, wpp, lng, lnb, *, tl):
    bsz, seq, d = xn.shape
    row = lambda a: pl.BlockSpec((bsz, tl, a.shape[-1]), lambda i: (0, i, 0))
    full = lambda a: pl.BlockSpec(a.shape, lambda i: (0,) * a.ndim)
    y_spec = pl.BlockSpec((yg.shape[0], tl // S5_CHUNK * bsz, yg.shape[2]), lambda i: (0, i, 0))
    params = (dskip, wglu, bglu, wout, wpg, wpp, lng, lnb)
    return pl.pallas_call(
        _out_kernel,
        grid=(seq // tl,),
        in_specs=[row(xn), row(p3), y_spec, row(xs), row(gs), row(o), row(gm)]
                 + [full(a) for a in params],
        out_specs=pl.BlockSpec((bsz, tl, d), lambda i: (0, i, 0)),
        out_shape=jax.ShapeDtypeStruct((bsz, seq, d), F32),
        scratch_shapes=[pltpu.VMEM((xs.shape[-1] // LANES, bsz * tl, LANES), F32)],
        compiler_params=pltpu.CompilerParams(dimension_semantics=("arbitrary",),
                                             vmem_limit_bytes=VMEM_LIMIT),
        name="out_proj",
    )(xn, p3, yg, xs, gs, o, gm, *params)


def _swap_halves_cols(w):
    half = w.shape[-1] // 2
    return jnp.concatenate([w[..., half:], w[..., :half]], axis=-1)


def kernel(x, p, positions, ln_emb_g, ln_emb_b, w_in, a_re, a_im, log_dt, b_re, b_im, c_re,
           c_im, d_skip, w_glu, b_glu, q_norm_g, w_uq, kv_norm_g, w_ukv, w_out, w_pg, w_pp,
           ln_g, ln_b):
    bsz, seq, d = x.shape
    assert bsz == SUBLANES, "the S5 recurrence keeps the batch on the 8 sublanes of a vreg"
    sw = d_skip.shape[-1]
    mw = MLA_HEADS * V_HEAD
    row2 = lambda a: a.reshape(1, -1)

    wi = w_in[0]
    o_cq = 2 * sw
    o_ckv = o_cq + Q_LORA
    o_kr = o_ckv + KV_LORA
    o_gm = o_kr + QK_ROPE
    kr_cols = wi[:, o_kr:o_gm]
    win = jnp.concatenate([wi[:, 0:sw], wi[:, sw:2 * sw], wi[:, o_gm:o_gm + mw],
                           wi[:, o_cq:o_ckv], wi[:, o_ckv:o_kr], kr_cols,
                           _swap_halves_cols(kr_cols)], axis=1).astype(BF16)
    wq3 = w_uq[0].reshape(Q_LORA, MLA_HEADS, QK_NOPE + QK_ROPE)
    wq = jnp.concatenate([wq3, _swap_halves_cols(wq3[..., QK_NOPE:])], axis=-1)
    wq = wq.reshape(Q_LORA, MLA_HEADS * HEAD_PAD).astype(BF16)
    wkv3 = w_ukv[0].reshape(KV_LORA, MLA_HEADS, QK_NOPE + V_HEAD)
    wkv = jnp.concatenate([wkv3[..., :QK_NOPE].reshape(KV_LORA, -1),
                           wkv3[..., QK_NOPE:].reshape(KV_LORA, -1)], axis=1).astype(BF16)

    inv_freq = 1.0 / (ROPE_THETA ** (np.arange(0, QK_ROPE, 2, dtype=np.float32) / QK_ROPE))
    n_rep = LANES // inv_freq.size
    rope_phase = np.repeat(np.array([np.pi / 2, 0.0, np.pi / 2, 0.0], np.float32), inv_freq.size)
    invf = jnp.asarray(np.stack([np.tile(inv_freq, n_rep), rope_phase]), F32)
    half = LANES // 2
    pos_b = jnp.concatenate(
        [jnp.broadcast_to(positions[:bsz // 2, :, None], (bsz // 2, seq, half)),
         jnp.broadcast_to(positions[bsz // 2:, :, None], (bsz // 2, seq, half))], axis=-1)

    xn, xs, u, gs, gm, q, k, v = _in_proj_call(
        x, pos_b, invf, row2(ln_emb_g), row2(ln_emb_b), win, row2(q_norm_g[0]), wq,
        row2(kv_norm_g[0]), wkv, tl=ROW_TILE_STEPS)

    tmat, wst, wot, avec = _s5_prep_call(a_re[0], a_im[0], log_dt[0], b_re[0], b_im[0],
                                         c_re[0], c_im[0], gb=S5_GROUP_BLOCK)
    yg = _s5_call(u, tmat, wst, wot, avec, bsz=bsz, gb=S5_GROUP_BLOCK)

    o = _attn_call(q, k, v, tq=ATTN_Q_TILE, tk=ATTN_K_TILE)

    return _out_call(
        xn, p[0], yg, xs, gs, o, gm,
        row2(d_skip[0]), w_glu[0].astype(BF16), row2(b_glu[0]), w_out[0].astype(BF16),
        w_pg[0].astype(BF16), w_pp[0].astype(BF16), row2(ln_g[0]), row2(ln_b[0]),
        tl=ROW_TILE_STEPS)
```
